```python
import jax, jax.numpy as jnp
from jax import lax
import numpy as np


D_MODEL = 2048
BATCH = 4
SEQ = 4096
DEPTH = 1

M_HEADS = 4
M_WIDTH = D_MODEL // 2
M_HEAD_DIM = M_WIDTH // M_HEADS
M_CHUNK = 128
CONV_WIDTH = 4
G_WIDTH = D_MODEL // 2
G_GROUPS = 8
G_GROUP_DIM = G_WIDTH // G_GROUPS
G_CHUNK = 128
N_GROUPS = 4
EXPERTS_PER_GROUP = 8
N_EXPERTS = N_GROUPS * EXPERTS_PER_GROUP
TOP_K = 2
D_EXPERT = 512
MOE_BLOCK = 128
PLE_DIM = 256
N_BRANCHES = 2
EPS = 1e-6
IN_COLS = 4 * M_WIDTH + 2 * M_HEADS + 2 * G_WIDTH + N_BRANCHES * D_MODEL

kernel_name = 'hybrid_mlstm_gmlp_hmoe_block'


def rms_norm(x, g):
    xf = x.astype(jnp.float32)
    y = xf * lax.rsqrt(jnp.mean(xf * xf, axis=-1, keepdims=True) + EPS)
    return (y * g.astype(jnp.float32)).astype(x.dtype)


def causal_conv(x, w, b):
    K, C = w.shape
    y = lax.conv_general_dilated(x, w[:, None, :].astype(x.dtype), window_strides=(1,),
                                 padding=[(K - 1, 0)], dimension_numbers=('NWC', 'WIO', 'NWC'),
                                 feature_group_count=C)
    return y + b


def mlstm_chunkwise(q, k, v, i_pre, f_pre):
    B, S, H, Dh = q.shape
    L = M_CHUNK
    NC = S // L
    f32 = jnp.float32

    def chunk(t):
        t = t.astype(f32).reshape((B, NC, L, H) + t.shape[3:])
        return jnp.moveaxis(t, 3, 1)

    q = chunk(q)
    k = chunk(k) * (Dh ** -0.5)
    v = chunk(v)
    ig = chunk(i_pre)
    lf = jax.nn.log_sigmoid(chunk(f_pre))
    b = jnp.cumsum(lf, axis=-1)
    g = b[..., -1]

    a = g[..., None] - b + ig
    m_loc = jnp.max(a, axis=-1)
    w_loc = jnp.exp(a - m_loc[..., None])
    kv_loc = jnp.einsum('bhcl,bhcld,bhcle->bhcde', w_loc, k, v)
    n_loc = jnp.einsum('bhcl,bhcld->bhcd', w_loc, k)

    def step(carry, xs):
        C, n, m = carry
        kv_c, n_c, m_c, g_c = xs
        m_new = jnp.maximum(g_c + m, m_c)
        s_old = jnp.exp(g_c + m - m_new)
        s_loc = jnp.exp(m_c - m_new)
        C_new = s_old[..., None, None] * C + s_loc[..., None, None] * kv_c
        n_new = s_old[..., None] * n + s_loc[..., None] * n_c
        return (C_new, n_new, m_new), (C, n, m)

    init = (jnp.zeros((B, H, Dh, Dh), f32), jnp.zeros((B, H, Dh), f32), jnp.zeros((B, H), f32))
    xs = (jnp.moveaxis(kv_loc, 2, 0), jnp.moveaxis(n_loc, 2, 0),
          jnp.moveaxis(m_loc, 2, 0), jnp.moveaxis(g, 2, 0))
    _, (C_prev, n_prev, m_prev) = lax.scan(step, init, xs)
    C_prev = jnp.moveaxis(C_prev, 0, 2)
    n_prev = jnp.moveaxis(n_prev, 0, 2)
    m_prev = jnp.moveaxis(m_prev, 0, 2)

    log_d = b[..., :, None] - b[..., None, :] + ig[..., None, :]
    causal = jnp.tril(jnp.ones((L, L), dtype=bool))
    log_d = jnp.where(causal, log_d, -jnp.inf)
    log_inter = b + m_prev[..., None]
    m = jnp.maximum(log_inter, jnp.max(log_d, axis=-1))
    d = jnp.exp(log_d - m[..., None])
    s_inter = jnp.exp(log_inter - m)
    qk = jnp.einsum('bhcld,bhcsd->bhcls', q, k) * d
    num = (jnp.einsum('bhcls,bhcse->bhcle', qk, v)
           + s_inter[..., None] * jnp.einsum('bhcld,bhcde->bhcle', q, C_prev))
    den = jnp.sum(qk, axis=-1) + s_inter * jnp.einsum('bhcld,bhcd->bhcl', q, n_prev)
    h = num / jnp.maximum(jnp.abs(den), jnp.exp(-m))[..., None]
    return jnp.moveaxis(h, 1, 3).reshape(B, S, H, Dh)


def head_norm(h, g):
    B, S, H, Dh = h.shape
    mu = jnp.mean(h, axis=-1, keepdims=True)
    var = jnp.mean(jnp.square(h - mu), axis=-1, keepdims=True)
    return ((h - mu) * lax.rsqrt(var + EPS)).reshape(B, S, H * Dh) * g.astype(jnp.float32)


def spatial_gating(u, v, ln_g, ln_b, w_s, b_s):
    B, S, _ = v.shape
    NCk = S // G_CHUNK
    vf = v.astype(jnp.float32)
    mu = jnp.mean(vf, axis=-1, keepdims=True)
    var = jnp.mean(jnp.square(vf - mu), axis=-1, keepdims=True)
    vn = (vf - mu) * lax.rsqrt(var + EPS) * ln_g.astype(jnp.float32) + ln_b.astype(jnp.float32)
    vn = vn.reshape(B, NCk, G_CHUNK, G_GROUPS, G_GROUP_DIM)
    causal = jnp.tril(jnp.ones((G_CHUNK, G_CHUNK), dtype=bool))
    w = jnp.where(causal, w_s.astype(jnp.float32), 0.0)
    mixed = (jnp.einsum('gts,bcsge->bctge', w, vn)
             + jnp.transpose(b_s.astype(jnp.float32))[None, None, :, :, None])
    return u * mixed.reshape(B, S, G_WIDTH).astype(u.dtype)


def token_mixers(hn, w_in, conv_w, conv_b, b_gate, gn_m, ln_g, ln_b, w_s, b_s, w_bm, w_bg, w_out):
    B, S, _ = hn.shape
    proj = hn @ w_in
    sizes = (M_WIDTH, M_WIDTH, M_WIDTH, M_WIDTH, 2 * M_HEADS, G_WIDTH, G_WIDTH)
    cuts = [int(c) for c in np.cumsum(sizes)]
    q, k, v, o_pre, if_pre, u, vg, gates = jnp.split(proj, cuts, axis=-1)
    qk = jax.nn.silu(causal_conv(jnp.concatenate([q, k], axis=-1), conv_w, conv_b))
    q, k = jnp.split(qk, 2, axis=-1)
    heads = lambda t: t.reshape(B, S, M_HEADS, M_HEAD_DIM)
    gate_pre = if_pre.astype(jnp.float32) + b_gate.astype(jnp.float32)
    h = mlstm_chunkwise(heads(q), heads(k), heads(v), gate_pre[..., :M_HEADS], gate_pre[..., M_HEADS:])
    h_m = (jax.nn.sigmoid(o_pre.astype(jnp.float32)) * head_norm(h, gn_m)).astype(hn.dtype)
    h_g = spatial_gating(jax.nn.gelu(u), jax.nn.gelu(vg), ln_g, ln_b, w_s, b_s)
    g_m, g_g = jnp.split(jax.nn.sigmoid(gates), 2, axis=-1)
    merged = g_m * (h_m @ w_bm) + g_g * (h_g @ w_bg)
    return merged @ w_out


def hier_moe(h, w_rg, b_rg, w_re, b_re, w1, w3, w2):
    B, S, D = h.shape
    T = B * S
    f32 = jnp.float32
    hf = h.reshape(T, D)
    g_logits = (hf @ w_rg).astype(f32) + b_rg.astype(f32)
    g_prob = jax.nn.softmax(g_logits, axis=-1)
    grp = jnp.argmax(g_logits, axis=-1)
    p_grp = jnp.take_along_axis(g_prob, grp[:, None], axis=-1)
    e_logits = ((hf @ w_re).astype(f32) + b_re.astype(f32)).reshape(T, N_GROUPS, EXPERTS_PER_GROUP)
    e_logits = jnp.take_along_axis(e_logits, grp[:, None, None], axis=1)[:, 0]
    top_v, top_i = lax.top_k(e_logits, TOP_K)
    wts = jax.nn.softmax(top_v, axis=-1) * p_grp
    eid = grp[:, None] * EXPERTS_PER_GROUP + top_i

    A = T * TOP_K
    flat_e = eid.reshape(A)
    flat_w = wts.reshape(A)
    flat_t = jnp.repeat(jnp.arange(T, dtype=jnp.int32), TOP_K)
    order = jnp.argsort(flat_e)
    se, st, sw = flat_e[order], flat_t[order], flat_w[order]
    counts = jnp.bincount(flat_e, length=N_EXPERTS)
    starts = jnp.cumsum(counts) - counts
    padded = (counts + MOE_BLOCK - 1) // MOE_BLOCK * MOE_BLOCK
    pends = jnp.cumsum(padded)
    pstarts = pends - padded
    dest = pstarts[se] + jnp.arange(A) - starts[se]
    NB = -(-A // MOE_BLOCK) + N_EXPERTS
    P = NB * MOE_BLOCK
    tok_buf = jnp.full((P,), T, dtype=jnp.int32).at[dest].set(st)
    w_buf = jnp.zeros((P,), f32).at[dest].set(sw)
    blk_e = jnp.clip(jnp.searchsorted(pends, jnp.arange(NB) * MOE_BLOCK, side='right'), 0, N_EXPERTS - 1)

    h_pad = jnp.concatenate([hf, jnp.zeros((1, D), hf.dtype)], axis=0)
    xb = h_pad[tok_buf].reshape(NB, MOE_BLOCK, D)

    def expert_block(args):
        xblk, e = args
        return (jax.nn.silu(xblk @ w1[e]) * (xblk @ w3[e])) @ w2[e]

    yb = lax.map(expert_block, (xb, blk_e)).reshape(P, D)
    out = jnp.zeros((T + 1, D), f32).at[tok_buf].add(yb.astype(f32) * w_buf[:, None])[:T]
    return out.reshape(B, S, D).astype(h.dtype)


def setup_inputs(seed: int = 0) -> dict:
    key = jax.random.key(seed)
    ks = jax.random.split(key, 32)
    f32 = jnp.float32
    nrm = lambda k, shape, scale: jax.random.normal(k, shape, f32) * scale
    L = DEPTH
    x = nrm(ks[0], (BATCH, SEQ, D_MODEL), 1.0)
    p = nrm(ks[1], (DEPTH, BATCH, SEQ, PLE_DIM), 1.0)
    g_mix = 1.0 + nrm(ks[2], (L, D_MODEL), 0.02)
    w_in = nrm(ks[3], (L, D_MODEL, IN_COLS), D_MODEL ** -0.5)
    conv_w = nrm(ks[4], (L, CONV_WIDTH, 2 * M_WIDTH), CONV_WIDTH ** -0.5)
    conv_b = nrm(ks[5], (L, 2 * M_WIDTH), 0.02)
    b_i = nrm(ks[6], (L, M_HEADS), 0.1)
    b_f = jnp.linspace(3.0, 6.0, M_HEADS, dtype=f32)[None, :] + nrm(ks[7], (L, M_HEADS), 0.1)
    b_gate = jnp.concatenate([b_i, b_f], axis=-1)
    gn_m = 1.0 + nrm(ks[8], (L, M_WIDTH), 0.02)
    ln_g = 1.0 + nrm(ks[9], (L, G_WIDTH), 0.02)
    ln_b = nrm(ks[10], (L, G_WIDTH), 0.02)
    w_s = nrm(ks[11], (L, G_GROUPS, G_CHUNK, G_CHUNK), G_CHUNK ** -0.5)
    b_s = 1.0 + nrm(ks[12], (L, G_GROUPS, G_CHUNK), 0.02)
    w_bm = nrm(ks[13], (L, M_WIDTH, D_MODEL), M_WIDTH ** -0.5)
    w_bg = nrm(ks[14], (L, G_WIDTH, D_MODEL), G_WIDTH ** -0.5)
    w_out = nrm(ks[15], (L, D_MODEL, D_MODEL), D_MODEL ** -0.5)
    g_ffn = 1.0 + nrm(ks[16], (L, D_MODEL), 0.02)
    w_rg = nrm(ks[17], (L, D_MODEL, N_GROUPS), D_MODEL ** -0.5)
    b_rg = nrm(ks[18], (L, N_GROUPS), 0.01)
    w_re = nrm(ks[19], (L, D_MODEL, N_EXPERTS), D_MODEL ** -0.5)
    b_re = nrm(ks[20], (L, N_EXPERTS), 0.01)
    w1 = nrm(ks[21], (L, N_EXPERTS, D_MODEL, D_EXPERT), D_MODEL ** -0.5)
    w3 = nrm(ks[22], (L, N_EXPERTS, D_MODEL, D_EXPERT), D_MODEL ** -0.5)
    w2 = nrm(ks[23], (L, N_EXPERTS, D_EXPERT, D_MODEL), D_EXPERT ** -0.5)
    g_ple = 1.0 + nrm(ks[24], (L, D_MODEL), 0.02)
    w_ple_up = nrm(ks[25], (L, PLE_DIM, D_MODEL), PLE_DIM ** -0.5)
    w_ple_gate = nrm(ks[26], (L, D_MODEL, D_MODEL), D_MODEL ** -0.5)
    g_final = 1.0 + nrm(ks[27], (D_MODEL,), 0.02)
    return {'x': x, 'p': p, 'g_mix': g_mix, 'w_in': w_in, 'conv_w': conv_w, 'conv_b': conv_b,
            'b_gate': b_gate, 'gn_m': gn_m, 'ln_g': ln_g, 'ln_b': ln_b, 'w_s': w_s, 'b_s': b_s,
            'w_bm': w_bm, 'w_bg': w_bg, 'w_out': w_out, 'g_ffn': g_ffn, 'w_rg': w_rg, 'b_rg': b_rg,
            'w_re': w_re, 'b_re': b_re, 'w1': w1, 'w3': w3, 'w2': w2, 'g_ple': g_ple,
            'w_ple_up': w_ple_up, 'w_ple_gate': w_ple_gate, 'g_final': g_final}


def reference(x, p, g_mix, w_in, conv_w, conv_b, b_gate, gn_m, ln_g, ln_b, w_s, b_s,
              w_bm, w_bg, w_out, g_ffn, w_rg, b_rg, w_re, b_re, w1, w3, w2,
              g_ple, w_ple_up, w_ple_gate, g_final):
    for i in range(DEPTH):
        x = x + token_mixers(rms_norm(x, g_mix[i]), w_in[i], conv_w[i], conv_b[i], b_gate[i], gn_m[i],
                             ln_g[i], ln_b[i], w_s[i], b_s[i], w_bm[i], w_bg[i], w_out[i])
        x = x + hier_moe(rms_norm(x, g_ffn[i]), w_rg[i], b_rg[i], w_re[i], b_re[i], w1[i], w3[i], w2[i])
        ple = p[i] @ w_ple_up[i]
        x = x + jax.nn.sigmoid(rms_norm(x, g_ple[i]) @ w_ple_gate[i]) * ple
    return rms_norm(x, g_final)
```

```python
import functools

import jax
import jax.numpy as jnp
from jax import lax
from jax.experimental import pallas as pl
from jax.experimental.pallas import tpu as pltpu

EPS = 1e-6
M_CHUNK = 128
TOP_K = 2
LANES = 128
SUBLANES = 8
EXPERT_BLOCK = 256
MIB = 1024 * 1024

f32 = jnp.float32
bf16 = jnp.bfloat16


def _params(semantics, vmem_mib):
    return pltpu.CompilerParams(dimension_semantics=semantics, vmem_limit_bytes=vmem_mib * MIB)


def _const_spec(shape):
    nd = len(shape)
    return pl.BlockSpec(shape, lambda *_: (0,) * nd, pipeline_mode=pl.Buffered(1))


def _rms(x, g):
    ms = jnp.mean(x * x, axis=-1, keepdims=True)
    return x * lax.rsqrt(ms + EPS) * g


def _inproj_kernel(x_ref, g_ref, w_ref, o_ref, hn_ref):
    @pl.when(pl.program_id(1) == 0)
    def _():
        hn_ref[...] = _rms(x_ref[...], g_ref[...]).astype(hn_ref.dtype)

    o_ref[...] = jnp.dot(hn_ref[...], w_ref[...], preferred_element_type=f32)


def _col_tile(ncol, limit=1280):
    best = LANES
    for k in range(1, ncol // LANES + 1):
        if (ncol // LANES) % k == 0 and k * LANES <= limit:
            best = k * LANES
    return best


def _inproj(x2d, g, w):
    T, D = x2d.shape
    ncol = w.shape[1]
    tm = min(1024, T)
    tn = _col_tile(ncol)
    return pl.pallas_call(
        _inproj_kernel,
        grid=(T // tm, ncol // tn),
        in_specs=[pl.BlockSpec((tm, D), lambda i, j: (i, 0)),
                  pl.BlockSpec((1, D), lambda i, j: (0, 0)),
                  pl.BlockSpec((D, tn), lambda i, j: (0, j))],
        out_specs=pl.BlockSpec((tm, tn), lambda i, j: (i, j)),
        out_shape=jax.ShapeDtypeStruct((T, ncol), f32),
        scratch_shapes=[pltpu.VMEM((tm, D), bf16)],
        compiler_params=_params(("parallel", "arbitrary"), 52),
        name="inproj",
    )(x2d, g, w)


def _mlstm_kernel(q_ref, k_ref, v_ref, o_ref, if_ref, cw_ref, cb_ref, bg_ref, gn_ref, out_ref,
                  ext_ref, c_ref, n_ref, m_ref, *, H, Dh, KW):
    L = M_CHUNK
    W = H * Dh
    PAD = SUBLANES

    @pl.when(pl.program_id(1) == 0)
    def _():
        ext_ref[0:PAD, :] = jnp.zeros((PAD, 2 * W), f32)
        c_ref[...] = jnp.zeros_like(c_ref)
        n_ref[...] = jnp.zeros_like(n_ref)
        m_ref[...] = jnp.zeros_like(m_ref)

    ext_ref[PAD:PAD + L, 0:W] = q_ref[...]
    ext_ref[PAD:PAD + L, W:2 * W] = k_ref[...]

    def conv_silu(c0):
        acc = cb_ref[:, c0:c0 + Dh]
        for j in range(KW):
            acc = acc + cw_ref[j:j + 1, c0:c0 + Dh] * ext_ref[pl.ds(PAD - (KW - 1) + j, L), c0:c0 + Dh]
        return acc * jax.nn.sigmoid(acc)

    gp = if_ref[...] + bg_ref[...]
    lane = lax.broadcasted_iota(jnp.int32, (L, LANES), 1)
    lf = jnp.minimum(gp, 0.0) - jnp.log1p(jnp.exp(-jnp.abs(gp)))
    z = jnp.where(lane < H, gp, lf)
    row = lax.broadcasted_iota(jnp.int32, (L, L), 0)
    col = lax.broadcasted_iota(jnp.int32, (L, L), 1)
    causal = row >= col
    bc = jnp.dot(causal.astype(f32), z, precision=lax.Precision.HIGHEST, preferred_element_type=f32)
    zt = z.T
    bct = bc.T

    for h in range(H):
        qh = conv_silu(h * Dh)
        kh = conv_silu(W + h * Dh) * (Dh ** -0.5)
        vh = v_ref[:, h * Dh:(h + 1) * Dh]
        b_col = bc[:, H + h:H + h + 1]
        ig_col = z[:, h:h + 1]
        b_row = bct[H + h:H + h + 1, :]
        ig_row = zt[h:h + 1, :]
        g = b_col[L - 1:L, :]
        m_prev = m_ref[h, 0:1, 0:1]
        n_prev = n_ref[h, 0:1, :]

        a_col = g - b_col + ig_col
        m_loc = jnp.max(a_col, axis=0, keepdims=True)
        w_loc = jnp.exp(a_col - m_loc)

        log_d = jnp.where(causal, b_col - b_row + ig_row, -jnp.inf)
        log_inter = b_col + m_prev
        mt = jnp.maximum(log_inter, jnp.max(log_d, axis=-1, keepdims=True))
        d = jnp.exp(log_d - mt)
        s_inter = jnp.exp(log_inter - mt)

        qb = qh.astype(bf16)
        kb = kh.astype(bf16)
        vb = vh.astype(bf16)
        s = lax.dot_general(qb, kb, (((1,), (1,)), ((), ())), preferred_element_type=f32) * d
        num = (jnp.dot(s.astype(bf16), vb, preferred_element_type=f32)
               + s_inter * jnp.dot(qb, c_ref[h].astype(bf16), preferred_element_type=f32))
        den = (jnp.sum(s, axis=-1, keepdims=True)
               + s_inter * jnp.sum(qh * n_prev, axis=-1, keepdims=True))
        hh = num / jnp.maximum(jnp.abs(den), jnp.exp(-mt))

        m_new = jnp.maximum(g + m_prev, m_loc)
        s_old = jnp.exp(g + m_prev - m_new)
        s_loc = jnp.exp(m_loc - m_new)
        kw = kh * w_loc
        kv = jnp.dot(kw.T.astype(bf16), vb, preferred_element_type=f32)
        c_ref[h] = s_old * c_ref[h] + s_loc * kv
        n_ref[h, 0:1, :] = s_old * n_prev + s_loc * jnp.sum(kw, axis=0, keepdims=True)
        m_ref[h] = jnp.broadcast_to(m_new, m_ref.shape[1:])

        mu = jnp.mean(hh, axis=-1, keepdims=True)
        xc = hh - mu
        var = jnp.mean(xc * xc, axis=-1, keepdims=True)
        hn = xc * lax.rsqrt(var + EPS) * gn_ref[:, h * Dh:(h + 1) * Dh]
        out_ref[:, h * Dh:(h + 1) * Dh] = (jax.nn.sigmoid(o_ref[:, h * Dh:(h + 1) * Dh]) * hn).astype(out_ref.dtype)

    ext_ref[0:PAD, :] = ext_ref[L:L + PAD, :]


def _mlstm(proj, conv_w, conv_b, bg_pad, gn_m, *, B, S, H, W, if_col):
    T = B * S
    L = M_CHUNK
    NC = S // L
    Dh = W // H
    KW = conv_w.shape[0]
    rows = lambda b, c: b * NC + c
    return pl.pallas_call(
        functools.partial(_mlstm_kernel, H=H, Dh=Dh, KW=KW),
        grid=(B, NC),
        in_specs=[pl.BlockSpec((L, W), lambda b, c: (rows(b, c), 0)),
                  pl.BlockSpec((L, W), lambda b, c: (rows(b, c), 1)),
                  pl.BlockSpec((L, W), lambda b, c: (rows(b, c), 2)),
                  pl.BlockSpec((L, W), lambda b, c: (rows(b, c), 3)),
                  pl.BlockSpec((L, LANES), lambda b, c: (rows(b, c), if_col // LANES)),
                  pl.BlockSpec((KW, 2 * W), lambda b, c: (0, 0)),
                  pl.BlockSpec((1, 2 * W), lambda b, c: (0, 0)),
                  pl.BlockSpec((1, LANES), lambda b, c: (0, 0)),
                  pl.BlockSpec((1, W), lambda b, c: (0, 0))],
        out_specs=pl.BlockSpec((L, W), lambda b, c: (rows(b, c), 0)),
        out_shape=jax.ShapeDtypeStruct((T, W), bf16),
        scratch_shapes=[pltpu.VMEM((L + 2 * SUBLANES, 2 * W), f32),
                        pltpu.VMEM((H, Dh, Dh), f32),
                        pltpu.VMEM((H, SUBLANES, Dh), f32),
                        pltpu.VMEM((H, SUBLANES, LANES), f32)],
        compiler_params=_params(("parallel", "arbitrary"), 32),
        name="mlstm",
    )(proj, proj, proj, proj, proj, conv_w, conv_b, bg_pad, gn_m)


def _gmlp_kernel(u_ref, vg_ref, lng_ref, lnb_ref, ws_ref, bst_ref, out_ref, *, G, Gd):
    L = ws_ref.shape[1]
    vg = jax.nn.gelu(vg_ref[...])
    mu = jnp.mean(vg, axis=-1, keepdims=True)
    xc = vg - mu
    var = jnp.mean(xc * xc, axis=-1, keepdims=True)
    vn = (xc * lax.rsqrt(var + EPS) * lng_ref[...] + lnb_ref[...]).astype(bf16)
    row = lax.broadcasted_iota(jnp.int32, (L, L), 0)
    col = lax.broadcasted_iota(jnp.int32, (L, L), 1)
    causal = row >= col
    for g in range(G):
        sl = slice(g * Gd, (g + 1) * Gd)
        w = jnp.where(causal, ws_ref[g], 0.0).astype(bf16)
        mixed = jnp.dot(w, vn[:, sl], preferred_element_type=f32) + bst_ref[:, g:g + 1]
        out_ref[:, sl] = (jax.nn.gelu(u_ref[:, sl]) * mixed).astype(out_ref.dtype)


def _gmlp(proj, ln_g, ln_b, w_s, b_st, *, T, GW, u_col):
    G, L, _ = w_s.shape
    Gd = GW // G
    ub = u_col // GW
    return pl.pallas_call(
        functools.partial(_gmlp_kernel, G=G, Gd=Gd),
        grid=(T // L,),
        in_specs=[pl.BlockSpec((L, GW), lambda i: (i, ub)),
                  pl.BlockSpec((L, GW), lambda i: (i, ub + 1)),
                  pl.BlockSpec((1, GW), lambda i: (0, 0)),
                  pl.BlockSpec((1, GW), lambda i: (0, 0)),
                  pl.BlockSpec((G, L, L), lambda i: (0, 0, 0)),
                  pl.BlockSpec((L, G), lambda i: (0, 0))],
        out_specs=pl.BlockSpec((L, GW), lambda i: (i, 0)),
        out_shape=jax.ShapeDtypeStruct((T, GW), bf16),
        compiler_params=_params(("parallel",), 32),
        name="gmlp",
    )(proj, proj, ln_g, ln_b, w_s, b_st)


def _merge_kernel(hm_ref, hg_ref, gm_ref, gg_ref, x_ref, wbm_ref, wbg_ref, wout_ref, gffn_ref, wr_ref, br_ref,
                  x1_ref, hn2_ref, lg_ref):
    a = jnp.dot(hm_ref[...], wbm_ref[...], preferred_element_type=f32)
    b = jnp.dot(hg_ref[...], wbg_ref[...], preferred_element_type=f32)
    merged = jax.nn.sigmoid(gm_ref[...]) * a + jax.nn.sigmoid(gg_ref[...]) * b
    x1 = x_ref[...] + jnp.dot(merged.astype(bf16), wout_ref[...], preferred_element_type=f32)
    x1_ref[...] = x1
    hn2 = _rms(x1, gffn_ref[...])
    hn2_ref[...] = hn2
    lg_ref[...] = jnp.dot(hn2.astype(bf16), wr_ref[...], preferred_element_type=f32) + br_ref[...]


def _merge(h_m, h_g, proj, x2d, w_bm, w_bg, w_out, g_ffn, w_r, b_r, *, gate_col):
    T, D = x2d.shape
    W = h_m.shape[1]
    GW = h_g.shape[1]
    tm = min(256, T)
    gb = gate_col // D
    row = lambda i: (i, 0)
    return pl.pallas_call(
        _merge_kernel,
        grid=(T // tm,),
        in_specs=[pl.BlockSpec((tm, W), row),
                  pl.BlockSpec((tm, GW), row),
                  pl.BlockSpec((tm, D), lambda i: (i, gb)),
                  pl.BlockSpec((tm, D), lambda i: (i, gb + 1)),
                  pl.BlockSpec((tm, D), row),
                  _const_spec((W, D)), _const_spec((GW, D)), _const_spec((D, D)),
                  _const_spec((1, D)), _const_spec((D, LANES)), _const_spec((1, LANES))],
        out_specs=[pl.BlockSpec((tm, D), row), pl.BlockSpec((tm, D), row), pl.BlockSpec((tm, LANES), row)],
        out_shape=[jax.ShapeDtypeStruct((T, D), f32), jax.ShapeDtypeStruct((T, D), f32),
                   jax.ShapeDtypeStruct((T, LANES), f32)],
        compiler_params=_params(("parallel",), 48),
        name="merge",
    )(h_m, h_g, proj, proj, x2d, w_bm, w_bg, w_out, g_ffn, w_r, b_r)


def _route_kernel(lg_ref, info_ref, cnt_ref, carry_ref, *, NG, EPG):
    tm = lg_ref.shape[0]

    @pl.when(pl.program_id(0) == 0)
    def _():
        carry_ref[...] = jnp.zeros_like(carry_ref)

    lg = lg_ref[...]
    lane = lax.broadcasted_iota(jnp.int32, (tm, LANES), 1)
    lanef = lane.astype(f32)
    big = float(LANES)

    def first_max(vals):
        mx = jnp.max(vals, axis=-1, keepdims=True)
        idx = jnp.min(jnp.where(vals == mx, lanef, big), axis=-1, keepdims=True)
        return mx, idx

    is_g = lane < NG
    gmax, grp = first_max(jnp.where(is_g, lg, -jnp.inf))
    p_grp = 1.0 / jnp.sum(jnp.where(is_g, jnp.exp(lg - gmax), 0.0), axis=-1, keepdims=True)
    lo = NG + grp * EPG
    el = jnp.where(jnp.logical_and(lanef >= lo, lanef < lo + EPG), lg, -jnp.inf)
    v1, i1 = first_max(el)
    v2, i2 = first_max(jnp.where(lanef == i1, -jnp.inf, el))
    t = jnp.exp(v2 - v1)
    w1 = p_grp / (1.0 + t)
    w2 = p_grp * t / (1.0 + t)

    hit1 = lanef == i1
    hit2 = lanef == i2
    oh = jnp.where(jnp.logical_or(hit1, hit2), 1.0, 0.0)
    r = lax.broadcasted_iota(jnp.int32, (tm, tm), 0)
    c = lax.broadcasted_iota(jnp.int32, (tm, tm), 1)
    before = jnp.where(r > c, 1.0, 0.0).astype(bf16)
    carry = carry_ref[0:1, :]
    earlier = jnp.dot(before, oh.astype(bf16), preferred_element_type=f32) + carry
    rank1 = jnp.sum(jnp.where(hit1, earlier, 0.0), axis=-1, keepdims=True)
    rank2 = jnp.sum(jnp.where(hit2, earlier, 0.0), axis=-1, keepdims=True)
    new_carry = carry + jnp.sum(oh, axis=0, keepdims=True)
    carry_ref[...] = jnp.broadcast_to(new_carry, carry_ref.shape)
    cnt_ref[...] = jnp.broadcast_to(new_carry, cnt_ref.shape)

    vals = (i1 - NG, i2 - NG, w1, w2, rank1, rank2)
    info = jnp.zeros((tm, LANES), f32)
    for k, v in enumerate(vals):
        info = jnp.where(lane == k, v, info)
    info_ref[...] = info


def _route(logits, *, NG, EPG):
    T = logits.shape[0]
    tm = min(512, T)
    return pl.pallas_call(
        functools.partial(_route_kernel, NG=NG, EPG=EPG),
        grid=(T // tm,),
        in_specs=[pl.BlockSpec((tm, LANES), lambda i: (i, 0))],
        out_specs=[pl.BlockSpec((tm, LANES), lambda i: (i, 0)),
                   pl.BlockSpec((SUBLANES, LANES), lambda i: (0, 0))],
        out_shape=[jax.ShapeDtypeStruct((T, LANES), f32), jax.ShapeDtypeStruct((SUBLANES, LANES), f32)],
        scratch_shapes=[pltpu.VMEM((SUBLANES, LANES), f32)],
        compiler_params=_params(("arbitrary",), 32),
        name="route",
    )(logits)


def _dest_kernel(info_ref, ps_ref, dest_ref):
    tm = info_ref.shape[0]
    info = info_ref[...]
    lane = lax.broadcasted_iota(jnp.int32, (tm, LANES), 1)
    lanef = lane.astype(f32)
    ps = ps_ref[...]
    d1 = jnp.sum(jnp.where(lanef == info[:, 0:1], ps, 0.0), axis=-1, keepdims=True) + info[:, 4:5]
    d2 = jnp.sum(jnp.where(lanef == info[:, 1:2], ps, 0.0), axis=-1, keepdims=True) + info[:, 5:6]
    dest_ref[...] = jnp.where(lane == 0, d1, jnp.where(lane == 1, d2, 0.0)).astype(jnp.int32)


def _dest(info, pstart_row):
    T = info.shape[0]
    tm = min(512, T)
    return pl.pallas_call(
        _dest_kernel,
        grid=(T // tm,),
        in_specs=[pl.BlockSpec((tm, LANES), lambda i: (i, 0)), pl.BlockSpec((1, LANES), lambda i: (0, 0))],
        out_specs=pl.BlockSpec((tm, LANES), lambda i: (i, 0)),
        out_shape=jax.ShapeDtypeStruct((T, LANES), jnp.int32),
        compiler_params=_params(("parallel",), 32),
        name="dest",
    )(info, pstart_row)


def _row_copy(src, si, dst, di, sem):
    return pltpu.make_async_copy(src.at[pl.ds(si, 1), :], dst.at[pl.ds(di, 1), :], sem)


def _dispatch_kernel(dest_ref, hn_ref, xs_in_ref, xs_ref, sem):
    del xs_in_ref
    tm = hn_ref.shape[0]

    def issue(r, carry):
        for k in range(TOP_K):
            _row_copy(hn_ref, r, xs_ref, dest_ref[0, k, r], sem).start()
        return carry

    lax.fori_loop(0, tm, issue, 0, unroll=8)

    def drain(r, carry):
        for k in range(TOP_K):
            _row_copy(hn_ref, 0, xs_ref, 0, sem).wait()
        return carry

    lax.fori_loop(0, tm, drain, 0, unroll=8)


def _dispatch(dest_blocks, hn2, xs_zero):
    T, D = hn2.shape
    tm = dest_blocks.shape[2]
    return pl.pallas_call(
        _dispatch_kernel,
        grid=(T // tm,),
        in_specs=[pl.BlockSpec((1, TOP_K, tm), lambda i: (i, 0, 0), memory_space=pltpu.SMEM),
                  pl.BlockSpec((tm, D), lambda i: (i, 0)),
                  pl.BlockSpec(memory_space=pl.ANY)],
        out_specs=pl.BlockSpec(memory_space=pl.ANY),
        out_shape=jax.ShapeDtypeStruct(xs_zero.shape, xs_zero.dtype),
        scratch_shapes=[pltpu.SemaphoreType.DMA(())],
        input_output_aliases={2: 0},
        compiler_params=_params(("arbitrary",), 32),
        name="dispatch",
    )(dest_blocks, hn2, xs_zero)


def _experts_kernel(be_ref, nu_ref, xs_ref, w1_ref, w3_ref, w2_ref, ys_ref, wb1, wb3, wb2):
    i = pl.program_id(0)
    e = be_ref[i]
    prev = be_ref[jnp.maximum(i - 1, 0)]

    @pl.when(jnp.logical_or(i == 0, e != prev))
    def _():
        wb1[...] = w1_ref[...].astype(bf16)
        wb3[...] = w3_ref[...].astype(bf16)
        wb2[...] = w2_ref[...].astype(bf16)

    @pl.when(i < nu_ref[0])
    def _():
        x = xs_ref[...].astype(bf16)
        h1 = jnp.dot(x, wb1[...], preferred_element_type=f32)
        h3 = jnp.dot(x, wb3[...], preferred_element_type=f32)
        a = (h1 * jax.nn.sigmoid(h1) * h3).astype(bf16)
        ys_ref[...] = jnp.dot(a, wb2[...], preferred_element_type=f32)

    @pl.when(i >= nu_ref[0])
    def _():
        ys_ref[...] = jnp.zeros_like(ys_ref)


def _experts(blk_e, n_used, xs, w1, w3, w2):
    P, D = xs.shape
    NE, _, De = w1.shape
    BM = EXPERT_BLOCK
    grid_spec = pltpu.PrefetchScalarGridSpec(
        num_scalar_prefetch=2,
        grid=(P // BM,),
        in_specs=[pl.BlockSpec((BM, D), lambda i, be, nu: (i, 0)),
                  pl.BlockSpec((None, D, De), lambda i, be, nu: (be[i], 0, 0)),
                  pl.BlockSpec((None, D, De), lambda i, be, nu: (be[i], 0, 0)),
                  pl.BlockSpec((None, De, D), lambda i, be, nu: (be[i], 0, 0))],
        out_specs=pl.BlockSpec((BM, D), lambda i, be, nu: (i, 0)),
        scratch_shapes=[pltpu.VMEM((D, De), bf16), pltpu.VMEM((D, De), bf16), pltpu.VMEM((De, D), bf16)],
    )
    return pl.pallas_call(
        _experts_kernel,
        grid_spec=grid_spec,
        out_shape=jax.ShapeDtypeStruct((P, D), f32),
        compiler_params=_params(("arbitrary",), 52),
        name="experts",
    )(blk_e, n_used, xs, w1, w3, w2)


def _final_kernel(dest_ref, x1_ref, p_ref, info_ref, ys_ref, wup_ref, wg_ref, gple_ref, gfin_ref, out_ref,
                  buf_ref, sem, *, final_norm):
    tm = x1_ref.shape[0]

    def issue(r, carry):
        for k in range(TOP_K):
            _row_copy(ys_ref, dest_ref[0, k, r], buf_ref.at[k], r, sem).start()
        return carry

    lax.fori_loop(0, tm, issue, 0, unroll=8)
    ple = jnp.dot(p_ref[...].astype(bf16), wup_ref[...], preferred_element_type=f32)

    def drain(r, carry):
        for k in range(TOP_K):
            _row_copy(ys_ref, 0, buf_ref.at[k], 0, sem).wait()
        return carry

    lax.fori_loop(0, tm, drain, 0, unroll=8)

    info = info_ref[...]
    moe = buf_ref[0] * info[:, 2:3] + buf_ref[1] * info[:, 3:4]
    x2 = x1_ref[...] + moe
    hn3 = _rms(x2, gple_ref[...]).astype(bf16)
    gate = jax.nn.sigmoid(jnp.dot(hn3, wg_ref[...], preferred_element_type=f32))
    x3 = x2 + gate * ple
    out_ref[...] = _rms(x3, gfin_ref[...]) if final_norm else x3


def _final(dest_blocks, x1, p2d, info, ys, w_up, w_gate, g_ple, g_final, *, final_norm):
    T, D = x1.shape
    PD = p2d.shape[1]
    tm = dest_blocks.shape[2]
    row = lambda i: (i, 0)
    return pl.pallas_call(
        functools.partial(_final_kernel, final_norm=final_norm),
        grid=(T // tm,),
        in_specs=[pl.BlockSpec((1, TOP_K, tm), lambda i: (i, 0, 0), memory_space=pltpu.SMEM),
                  pl.BlockSpec((tm, D), row),
                  pl.BlockSpec((tm, PD), row),
                  pl.BlockSpec((tm, LANES), row),
                  pl.BlockSpec(memory_space=pl.ANY),
                  _const_spec((PD, D)), _const_spec((D, D)), _const_spec((1, D)), _const_spec((1, D))],
        out_specs=pl.BlockSpec((tm, D), row),
        out_shape=jax.ShapeDtypeStruct((T, D), f32),
        scratch_shapes=[pltpu.VMEM((TOP_K, tm, D), f32), pltpu.SemaphoreType.DMA(())],
        compiler_params=_params(("arbitrary",), 40),
        name="final",
    )(dest_blocks, x1, p2d, info, ys, w_up, w_gate, g_ple, g_final)


def _layer(x2d, p2d, g_mix, w_in, conv_w, conv_b, b_gate, gn_m, ln_g, ln_b, w_s, b_s, w_bm, w_bg, w_out,
           g_ffn, w_rg, b_rg, w_re, b_re, w1, w3, w2, g_ple, w_ple_up, w_ple_gate, g_out, *, B, S, final_norm):
    T, D = x2d.shape
    H = b_gate.shape[0] // 2
    W = gn_m.shape[0]
    GW = ln_g.shape[0]
    NG = w_rg.shape[1]
    NE = w_re.shape[1]
    EPG = NE // NG
    BM = EXPERT_BLOCK
    row2 = lambda v: v.reshape(1, -1).astype(f32)

    n_if = 2 * H
    c_if = 4 * W
    c_uv = c_if + n_if
    w_perm = jnp.concatenate(
        [w_in[:, :c_if], w_in[:, c_uv:], w_in[:, c_if:c_uv], jnp.zeros((D, LANES - n_if), w_in.dtype)],
        axis=1).astype(bf16)
    u_col = 4 * W
    gate_col = u_col + 2 * GW
    if_col = gate_col + 2 * D
    bg_pad = jnp.concatenate([b_gate.astype(f32), jnp.zeros((LANES - n_if,), f32)]).reshape(1, LANES)

    proj = _inproj(x2d, row2(g_mix), w_perm)
    h_m = _mlstm(proj, conv_w.astype(f32), row2(conv_b), bg_pad, row2(gn_m), B=B, S=S, H=H, W=W, if_col=if_col)
    h_g = _gmlp(proj, row2(ln_g), row2(ln_b), w_s.astype(f32), jnp.transpose(b_s).astype(f32),
                T=T, GW=GW, u_col=u_col)

    w_r = jnp.concatenate([w_rg, w_re, jnp.zeros((D, LANES - NG - NE), w_rg.dtype)], axis=1).astype(bf16)
    b_r = jnp.concatenate([b_rg.astype(f32), b_re.astype(f32), jnp.zeros((LANES - NG - NE,), f32)]).reshape(1, LANES)
    x1, hn2, logits = _merge(h_m, h_g, proj, x2d, w_bm.astype(bf16), w_bg.astype(bf16), w_out.astype(bf16),
                             row2(g_ffn), w_r, b_r, gate_col=gate_col)

    info, cnt = _route(logits, NG=NG, EPG=EPG)
    counts = cnt[0, NG:NG + NE].astype(jnp.int32)
    padded = (counts + BM - 1) // BM * BM
    pends = jnp.cumsum(padded)
    pstarts = pends - padded
    NB = (T * TOP_K) // BM + NE
    blk_e = jnp.clip(jnp.searchsorted(pends, jnp.arange(NB, dtype=jnp.int32) * BM, side='right'),
                     0, NE - 1).astype(jnp.int32)
    n_used = (pends[-1:] // BM).astype(jnp.int32)
    ps_row = jnp.zeros((1, LANES), f32).at[0, :NE].set(pstarts.astype(f32))

    dest = _dest(info, ps_row)
    tmd = min(256, T)
    dest_blocks = dest[:, :TOP_K].reshape(T // tmd, tmd, TOP_K).transpose(0, 2, 1)

    xs = _dispatch(dest_blocks, hn2, jnp.zeros((NB * BM, D), f32))
    ys = _experts(blk_e, n_used, xs, w1, w3, w2)
    return _final(dest_blocks, x1, p2d, info, ys, w_ple_up.astype(bf16), w_ple_gate.astype(bf16),
                  row2(g_ple), row2(g_out), final_norm=final_norm)


def kernel(x, p, g_mix, w_in, conv_w, conv_b, b_gate, gn_m, ln_g, ln_b, w_s, b_s, w_bm, w_bg, w_out, g_ffn, w_rg, b_rg, w_re, b_re, w1, w3, w2, g_ple, w_ple_up, w_ple_gate, g_final):
    B, S, D = x.shape
    depth = w_in.shape[0]
    x2d = x.reshape(B * S, D)
    for i in range(depth):
        last = i == depth - 1
        x2d = _layer(x2d, p[i].reshape(B * S, -1), g_mix[i], w_in[i], conv_w[i], conv_b[i], b_gate[i], gn_m[i],
                     ln_g[i], ln_b[i], w_s[i], b_s[i], w_bm[i], w_bg[i], w_out[i], g_ffn[i], w_rg[i], b_rg[i],
                     w_re[i], b_re[i], w1[i], w3[i], w2[i], g_ple[i], w_ple_up[i], w_ple_gate[i], g_final, B=B, S=S, final_norm=last)
    return x2d.reshape(B, S, D)
```

```python
import functools

import jax
import jax.numpy as jnp
from jax import lax
from jax.experimental import pallas as pl
from jax.experimental.pallas import tpu as pltpu

EPS = 1e-6
M_CHUNK = 128
TOP_K = 2
LANES = 128
SUBLANES = 8
EXPERT_BLOCK = 256
MIB = 1024 * 1024

f32 = jnp.float32
bf16 = jnp.bfloat16


def _params(semantics, vmem_mib):
    return pltpu.CompilerParams(dimension_semantics=semantics, vmem_limit_bytes=vmem_mib * MIB)


def _const_spec(shape):
    nd = len(shape)
    return pl.BlockSpec(shape, lambda *_: (0,) * nd, pipeline_mode=pl.Buffered(1))


def _rms(x, g):
    ms = jnp.mean(x * x, axis=-1, keepdims=True)
    return x * lax.rsqrt(ms + EPS) * g


def _inproj_kernel(x_ref, g_ref, wa_ref, wb_ref, wif_ref, o_ref, oif_ref, hn_ref, *, na):
    j = pl.program_id(1)

    @pl.when(j == 0)
    def _():
        hn_ref[...] = _rms(x_ref[...], g_ref[...]).astype(hn_ref.dtype)
        oif_ref[...] = jnp.dot(hn_ref[...], wif_ref[...], preferred_element_type=f32)

    @pl.when(j < na)
    def _():
        o_ref[...] = jnp.dot(hn_ref[...], wa_ref[...], preferred_element_type=f32)

    @pl.when(j >= na)
    def _():
        o_ref[...] = jnp.dot(hn_ref[...], wb_ref[...], preferred_element_type=f32)


def _inproj(x2d, g, w_a, w_b, w_if):
    T, D = x2d.shape
    tm = min(1024, T)
    tn = 1024
    na, nb = w_a.shape[1] // tn, w_b.shape[1] // tn
    return pl.pallas_call(
        functools.partial(_inproj_kernel, na=na),
        grid=(T // tm, na + nb),
        in_specs=[pl.BlockSpec((tm, D), lambda i, j: (i, 0)),
                  pl.BlockSpec((1, D), lambda i, j: (0, 0)),
                  pl.BlockSpec((D, tn), lambda i, j: (0, jnp.minimum(j, na - 1))),
                  pl.BlockSpec((D, tn), lambda i, j: (0, jnp.maximum(j - na, 0))),
                  pl.BlockSpec((D, LANES), lambda i, j: (0, 0))],
        out_specs=[pl.BlockSpec((tm, tn), lambda i, j: (i, j)),
                   pl.BlockSpec((tm, LANES), lambda i, j: (i, 0))],
        out_shape=[jax.ShapeDtypeStruct((T, (na + nb) * tn), f32), jax.ShapeDtypeStruct((T, LANES), f32)],
        scratch_shapes=[pltpu.VMEM((tm, D), bf16)],
        compiler_params=_params(("parallel", "arbitrary"), 52),
        name="inproj",
    )(x2d, g, w_a, w_b, w_if)


def _mlstm_kernel(q_ref, k_ref, v_ref, o_ref, if_ref, cw_ref, cb_ref, bg_ref, gn_ref, out_ref,
                  ext_ref, c_ref, n_ref, m_ref, *, H, Dh, KW):
    L = M_CHUNK
    W = H * Dh
    PAD = SUBLANES

    @pl.when(pl.program_id(1) == 0)
    def _():
        ext_ref[0:PAD, :] = jnp.zeros((PAD, 2 * W), f32)
        c_ref[...] = jnp.zeros_like(c_ref)
        n_ref[...] = jnp.zeros_like(n_ref)
        m_ref[...] = jnp.zeros_like(m_ref)

    ext_ref[PAD:PAD + L, 0:W] = q_ref[...]
    ext_ref[PAD:PAD + L, W:2 * W] = k_ref[...]

    def conv_silu(c0):
        acc = cb_ref[:, c0:c0 + Dh]
        for j in range(KW):
            acc = acc + cw_ref[j:j + 1, c0:c0 + Dh] * ext_ref[pl.ds(PAD - (KW - 1) + j, L), c0:c0 + Dh]
        return acc * jax.nn.sigmoid(acc)

    gp = if_ref[...] + bg_ref[...]
    lane = lax.broadcasted_iota(jnp.int32, (L, LANES), 1)
    lf = jnp.minimum(gp, 0.0) - jnp.log1p(jnp.exp(-jnp.abs(gp)))
    z = jnp.where(lane < H, gp, lf)
    row = lax.broadcasted_iota(jnp.int32, (L, L), 0)
    col = lax.broadcasted_iota(jnp.int32, (L, L), 1)
    causal = row >= col
    bc = jnp.dot(causal.astype(f32), z, precision=lax.Precision.HIGHEST, preferred_element_type=f32)
    zt = z.T
    bct = bc.T

    for h in range(H):
        qh = conv_silu(h * Dh)
        kh = conv_silu(W + h * Dh) * (Dh ** -0.5)
        vh = v_ref[:, h * Dh:(h + 1) * Dh]
        b_col = bc[:, H + h:H + h + 1]
        ig_col = z[:, h:h + 1]
        b_row = bct[H + h:H + h + 1, :]
        ig_row = zt[h:h + 1, :]
        g = b_col[L - 1:L, :]
        m_prev = m_ref[h, 0:1, 0:1]
        n_prev = n_ref[h, 0:1, :]

        a_col = g - b_col + ig_col
        m_loc = jnp.max(a_col, axis=0, keepdims=True)
        w_loc = jnp.exp(a_col - m_loc)

        log_d = jnp.where(causal, b_col - b_row + ig_row, -jnp.inf)
        log_inter = b_col + m_prev
        mt = jnp.maximum(log_inter, jnp.max(log_d, axis=-1, keepdims=True))
        d = jnp.exp(log_d - mt)
        s_inter = jnp.exp(log_inter - mt)

        qb = qh.astype(bf16)
        kb = kh.astype(bf16)
        vb = vh.astype(bf16)
        s = lax.dot_general(qb, kb, (((1,), (1,)), ((), ())), preferred_element_type=f32) * d
        num = (jnp.dot(s.astype(bf16), vb, preferred_element_type=f32)
               + s_inter * jnp.dot(qb, c_ref[h].astype(bf16), preferred_element_type=f32))
        den = (jnp.sum(s, axis=-1, keepdims=True)
               + s_inter * jnp.sum(qh * n_prev, axis=-1, keepdims=True))
        hh = num / jnp.maximum(jnp.abs(den), jnp.exp(-mt))

        m_new = jnp.maximum(g + m_prev, m_loc)
        s_old = jnp.exp(g + m_prev - m_new)
        s_loc = jnp.exp(m_loc - m_new)
        kw = kh * w_loc
        kv = jnp.dot(kw.T.astype(bf16), vb, preferred_element_type=f32)
        c_ref[h] = s_old * c_ref[h] + s_loc * kv
        n_ref[h, 0:1, :] = s_old * n_prev + s_loc * jnp.sum(kw, axis=0, keepdims=True)
        m_ref[h] = jnp.broadcast_to(m_new, m_ref.shape[1:])

        mu = jnp.mean(hh, axis=-1, keepdims=True)
        xc = hh - mu
        var = jnp.mean(xc * xc, axis=-1, keepdims=True)
        hn = xc * lax.rsqrt(var + EPS) * gn_ref[:, h * Dh:(h + 1) * Dh]
        out_ref[:, h * Dh:(h + 1) * Dh] = (jax.nn.sigmoid(o_ref[:, h * Dh:(h + 1) * Dh]) * hn).astype(out_ref.dtype)

    ext_ref[0:PAD, :] = ext_ref[L:L + PAD, :]


def _mlstm(proj, pif, conv_w, conv_b, bg_pad, gn_m, *, B, S, H, W):
    T = B * S
    L = M_CHUNK
    NC = S // L
    Dh = W // H
    KW = conv_w.shape[0]
    rows = lambda b, c: b * NC + c
    return pl.pallas_call(
        functools.partial(_mlstm_kernel, H=H, Dh=Dh, KW=KW),
        grid=(B, NC),
        in_specs=[pl.BlockSpec((L, W), lambda b, c: (rows(b, c), 0)),
                  pl.BlockSpec((L, W), lambda b, c: (rows(b, c), 1)),
                  pl.BlockSpec((L, W), lambda b, c: (rows(b, c), 2)),
                  pl.BlockSpec((L, W), lambda b, c: (rows(b, c), 3)),
                  pl.BlockSpec((L, LANES), lambda b, c: (rows(b, c), 0)),
                  pl.BlockSpec((KW, 2 * W), lambda b, c: (0, 0)),
                  pl.BlockSpec((1, 2 * W), lambda b, c: (0, 0)),
                  pl.BlockSpec((1, LANES), lambda b, c: (0, 0)),
                  pl.BlockSpec((1, W), lambda b, c: (0, 0))],
        out_specs=pl.BlockSpec((L, W), lambda b, c: (rows(b, c), 0)),
        out_shape=jax.ShapeDtypeStruct((T, W), bf16),
        scratch_shapes=[pltpu.VMEM((L + 2 * SUBLANES, 2 * W), f32),
                        pltpu.VMEM((H, Dh, Dh), f32),
                        pltpu.VMEM((H, SUBLANES, Dh), f32),
                        pltpu.VMEM((H, SUBLANES, LANES), f32)],
        compiler_params=_params(("parallel", "arbitrary"), 32),
        name="mlstm",
    )(proj, proj, proj, proj, pif, conv_w, conv_b, bg_pad, gn_m)


def _gmlp_kernel(u_ref, vg_ref, lng_ref, lnb_ref, ws_ref, bst_ref, out_ref, *, G, Gd):
    L = ws_ref.shape[1]
    vg = jax.nn.gelu(vg_ref[...])
    mu = jnp.mean(vg, axis=-1, keepdims=True)
    xc = vg - mu
    var = jnp.mean(xc * xc, axis=-1, keepdims=True)
    vn = (xc * lax.rsqrt(var + EPS) * lng_ref[...] + lnb_ref[...]).astype(bf16)
    row = lax.broadcasted_iota(jnp.int32, (L, L), 0)
    col = lax.broadcasted_iota(jnp.int32, (L, L), 1)
    causal = row >= col
    for g in range(G):
        sl = slice(g * Gd, (g + 1) * Gd)
        w = jnp.where(causal, ws_ref[g], 0.0).astype(bf16)
        mixed = jnp.dot(w, vn[:, sl], preferred_element_type=f32) + bst_ref[:, g:g + 1]
        out_ref[:, sl] = (jax.nn.gelu(u_ref[:, sl]) * mixed).astype(out_ref.dtype)


def _gmlp(proj, ln_g, ln_b, w_s, b_st, *, T, GW, u_col):
    G, L, _ = w_s.shape
    Gd = GW // G
    ub = u_col // GW
    return pl.pallas_call(
        functools.partial(_gmlp_kernel, G=G, Gd=Gd),
        grid=(T // L,),
        in_specs=[pl.BlockSpec((L, GW), lambda i: (i, ub)),
                  pl.BlockSpec((L, GW), lambda i: (i, ub + 1)),
                  pl.BlockSpec((1, GW), lambda i: (0, 0)),
                  pl.BlockSpec((1, GW), lambda i: (0, 0)),
                  pl.BlockSpec((G, L, L), lambda i: (0, 0, 0)),
                  pl.BlockSpec((L, G), lambda i: (0, 0))],
        out_specs=pl.BlockSpec((L, GW), lambda i: (i, 0)),
        out_shape=jax.ShapeDtypeStruct((T, GW), bf16),
        compiler_params=_params(("parallel",), 32),
        name="gmlp",
    )(proj, proj, ln_g, ln_b, w_s, b_st)


def _merge_kernel(hm_ref, hg_ref, gm_ref, gg_ref, x_ref, wbm_ref, wbg_ref, wout_ref, gffn_ref, wr_ref, br_ref,
                  x1_ref, hn2_ref, lg_ref):
    a = jnp.dot(hm_ref[...], wbm_ref[...], preferred_element_type=f32)
    b = jnp.dot(hg_ref[...], wbg_ref[...], preferred_element_type=f32)
    merged = jax.nn.sigmoid(gm_ref[...]) * a + jax.nn.sigmoid(gg_ref[...]) * b
    x1 = x_ref[...] + jnp.dot(merged.astype(bf16), wout_ref[...], preferred_element_type=f32)
    x1_ref[...] = x1
    hn2 = _rms(x1, gffn_ref[...])
    hn2_ref[...] = hn2
    lg_ref[...] = jnp.dot(hn2.astype(bf16), wr_ref[...], preferred_element_type=f32) + br_ref[...]


def _merge(h_m, h_g, proj, x2d, w_bm, w_bg, w_out, g_ffn, w_r, b_r, *, gate_col):
    T, D = x2d.shape
    W = h_m.shape[1]
    GW = h_g.shape[1]
    tm = min(256, T)
    gb = gate_col // D
    row = lambda i: (i, 0)
    return pl.pallas_call(
        _merge_kernel,
        grid=(T // tm,),
        in_specs=[pl.BlockSpec((tm, W), row),
                  pl.BlockSpec((tm, GW), row),
                  pl.BlockSpec((tm, D), lambda i: (i, gb)),
                  pl.BlockSpec((tm, D), lambda i: (i, gb + 1)),
                  pl.BlockSpec((tm, D), row),
                  _const_spec((W, D)), _const_spec((GW, D)), _const_spec((D, D)),
                  _const_spec((1, D)), _const_spec((D, LANES)), _const_spec((1, LANES))],
        out_specs=[pl.BlockSpec((tm, D), row), pl.BlockSpec((tm, D), row), pl.BlockSpec((tm, LANES), row)],
        out_shape=[jax.ShapeDtypeStruct((T, D), f32), jax.ShapeDtypeStruct((T, D), f32),
                   jax.ShapeDtypeStruct((T, LANES), f32)],
        compiler_params=_params(("parallel",), 48),
        name="merge",
    )(h_m, h_g, proj, proj, x2d, w_bm, w_bg, w_out, g_ffn, w_r, b_r)


def _route_kernel(lg_ref, info_ref, cnt_ref, carry_ref, *, NG, EPG):
    tm = lg_ref.shape[0]

    @pl.when(pl.program_id(0) == 0)
    def _():
        carry_ref[...] = jnp.zeros_like(carry_ref)

    lg = lg_ref[...]
    lane = lax.broadcasted_iota(jnp.int32, (tm, LANES), 1)
    lanef = lane.astype(f32)
    big = float(LANES)

    def first_max(vals):
        mx = jnp.max(vals, axis=-1, keepdims=True)
        idx = jnp.min(jnp.where(vals == mx, lanef, big), axis=-1, keepdims=True)
        return mx, idx

    is_g = lane < NG
    gmax, grp = first_max(jnp.where(is_g, lg, -jnp.inf))
    p_grp = 1.0 / jnp.sum(jnp.where(is_g, jnp.exp(lg - gmax), 0.0), axis=-1, keepdims=True)
    lo = NG + grp * EPG
    el = jnp.where(jnp.logical_and(lanef >= lo, lanef < lo + EPG), lg, -jnp.inf)
    v1, i1 = first_max(el)
    v2, i2 = first_max(jnp.where(lanef == i1, -jnp.inf, el))
    t = jnp.exp(v2 - v1)
    w1 = p_grp / (1.0 + t)
    w2 = p_grp * t / (1.0 + t)

    hit1 = lanef == i1
    hit2 = lanef == i2
    oh = jnp.where(jnp.logical_or(hit1, hit2), 1.0, 0.0)
    r = lax.broadcasted_iota(jnp.int32, (tm, tm), 0)
    c = lax.broadcasted_iota(jnp.int32, (tm, tm), 1)
    before = jnp.where(r > c, 1.0, 0.0).astype(bf16)
    carry = carry_ref[0:1, :]
    earlier = jnp.dot(before, oh.astype(bf16), preferred_element_type=f32) + carry
    rank1 = jnp.sum(jnp.where(hit1, earlier, 0.0), axis=-1, keepdims=True)
    rank2 = jnp.sum(jnp.where(hit2, earlier, 0.0), axis=-1, keepdims=True)
    new_carry = carry + jnp.sum(oh, axis=0, keepdims=True)
    carry_ref[...] = jnp.broadcast_to(new_carry, carry_ref.shape)
    cnt_ref[...] = jnp.broadcast_to(new_carry, cnt_ref.shape)

    vals = (i1 - NG, i2 - NG, w1, w2, rank1, rank2)
    info = jnp.zeros((tm, LANES), f32)
    for k, v in enumerate(vals):
        info = jnp.where(lane == k, v, info)
    info_ref[...] = info


def _route(logits, *, NG, EPG):
    T = logits.shape[0]
    tm = min(512, T)
    return pl.pallas_call(
        functools.partial(_route_kernel, NG=NG, EPG=EPG),
        grid=(T // tm,),
        in_specs=[pl.BlockSpec((tm, LANES), lambda i: (i, 0))],
        out_specs=[pl.BlockSpec((tm, LANES), lambda i: (i, 0)),
                   pl.BlockSpec((SUBLANES, LANES), lambda i: (0, 0))],
        out_shape=[jax.ShapeDtypeStruct((T, LANES), f32), jax.ShapeDtypeStruct((SUBLANES, LANES), f32)],
        scratch_shapes=[pltpu.VMEM((SUBLANES, LANES), f32)],
        compiler_params=_params(("arbitrary",), 32),
        name="route",
    )(logits)


def _dest_kernel(info_ref, ps_ref, dest_ref):
    tm = info_ref.shape[0]
    info = info_ref[...]
    lane = lax.broadcasted_iota(jnp.int32, (tm, LANES), 1)
    lanef = lane.astype(f32)
    ps = ps_ref[...]
    d1 = jnp.sum(jnp.where(lanef == info[:, 0:1], ps, 0.0), axis=-1, keepdims=True) + info[:, 4:5]
    d2 = jnp.sum(jnp.where(lanef == info[:, 1:2], ps, 0.0), axis=-1, keepdims=True) + info[:, 5:6]
    dest_ref[...] = jnp.where(lane == 0, d1, jnp.where(lane == 1, d2, 0.0)).astype(jnp.int32)


def _dest(info, pstart_row):
    T = info.shape[0]
    tm = min(512, T)
    return pl.pallas_call(
        _dest_kernel,
        grid=(T // tm,),
        in_specs=[pl.BlockSpec((tm, LANES), lambda i: (i, 0)), pl.BlockSpec((1, LANES), lambda i: (0, 0))],
        out_specs=pl.BlockSpec((tm, LANES), lambda i: (i, 0)),
        out_shape=jax.ShapeDtypeStruct((T, LANES), jnp.int32),
        compiler_params=_params(("parallel",), 32),
        name="dest",
    )(info, pstart_row)


def _row_copy(src, si, dst, di, sem):
    return pltpu.make_async_copy(src.at[pl.ds(si, 1), :], dst.at[pl.ds(di, 1), :], sem)


def _dispatch_kernel(seg_ref, dest_ref, hn_ref, xs_ref, zero_ref, sem, zsem):
    tm = hn_ref.shape[0]
    NE = seg_ref.shape[1]

    @pl.when(pl.program_id(0) == 0)
    def _():
        zero_ref[...] = jnp.zeros_like(zero_ref)

        def fill(e, carry):
            def one(r, c):
                _row_copy(zero_ref, 0, xs_ref, r, zsem).start()
                return c
            return lax.fori_loop(seg_ref[0, e], seg_ref[1, e], one, carry)

        lax.fori_loop(0, NE, fill, 0)

        def fill_done(e, carry):
            def one(r, c):
                _row_copy(zero_ref, 0, xs_ref, 0, zsem).wait()
                return c
            return lax.fori_loop(seg_ref[0, e], seg_ref[1, e], one, carry)

        lax.fori_loop(0, NE, fill_done, 0)

    def issue(r, carry):
        for k in range(TOP_K):
            _row_copy(hn_ref, r, xs_ref, dest_ref[0, k, r], sem).start()
        return carry

    lax.fori_loop(0, tm, issue, 0, unroll=8)

    def drain(r, carry):
        for k in range(TOP_K):
            _row_copy(hn_ref, 0, xs_ref, 0, sem).wait()
        return carry

    lax.fori_loop(0, tm, drain, 0, unroll=8)


def _dispatch(seg, dest_blocks, hn2, n_rows):
    T, D = hn2.shape
    tm = dest_blocks.shape[2]
    grid_spec = pltpu.PrefetchScalarGridSpec(
        num_scalar_prefetch=1,
        grid=(T // tm,),
        in_specs=[pl.BlockSpec((1, TOP_K, tm), lambda i, seg: (i, 0, 0), memory_space=pltpu.SMEM),
                  pl.BlockSpec((tm, D), lambda i, seg: (i, 0))],
        out_specs=pl.BlockSpec(memory_space=pl.ANY),
        scratch_shapes=[pltpu.VMEM((SUBLANES, D), hn2.dtype), pltpu.SemaphoreType.DMA(()),
                        pltpu.SemaphoreType.DMA(())],
    )
    return pl.pallas_call(
        _dispatch_kernel,
        grid_spec=grid_spec,
        out_shape=jax.ShapeDtypeStruct((n_rows, D), hn2.dtype),
        compiler_params=_params(("arbitrary",), 32),
        name="dispatch",
    )(seg, dest_blocks, hn2)


def _experts_kernel(be_ref, nu_ref, xs_ref, w1_ref, w3_ref, w2_ref, ys_ref, wb1, wb3, wb2):
    i = pl.program_id(0)
    e = be_ref[i]
    prev = be_ref[jnp.maximum(i - 1, 0)]

    @pl.when(jnp.logical_or(i == 0, e != prev))
    def _():
        wb1[...] = w1_ref[...].astype(bf16)
        wb3[...] = w3_ref[...].astype(bf16)
        wb2[...] = w2_ref[...].astype(bf16)

    @pl.when(i < nu_ref[0])
    def _():
        x = xs_ref[...].astype(bf16)
        h1 = jnp.dot(x, wb1[...], preferred_element_type=f32)
        h3 = jnp.dot(x, wb3[...], preferred_element_type=f32)
        a = (h1 * jax.nn.sigmoid(h1) * h3).astype(bf16)
        ys_ref[...] = jnp.dot(a, wb2[...], preferred_element_type=f32)

    @pl.when(i >= nu_ref[0])
    def _():
        ys_ref[...] = jnp.zeros_like(ys_ref)


def _experts(blk_e, n_used, xs, w1, w3, w2):
    P, D = xs.shape
    NE, _, De = w1.shape
    BM = EXPERT_BLOCK
    grid_spec = pltpu.PrefetchScalarGridSpec(
        num_scalar_prefetch=2,
        grid=(P // BM,),
        in_specs=[pl.BlockSpec((BM, D), lambda i, be, nu: (jnp.minimum(i, nu[0] - 1), 0)),
                  pl.BlockSpec((None, D, De), lambda i, be, nu: (be[i], 0, 0)),
                  pl.BlockSpec((None, D, De), lambda i, be, nu: (be[i], 0, 0)),
                  pl.BlockSpec((None, De, D), lambda i, be, nu: (be[i], 0, 0))],
        out_specs=pl.BlockSpec((BM, D), lambda i, be, nu: (i, 0)),
        scratch_shapes=[pltpu.VMEM((D, De), bf16), pltpu.VMEM((D, De), bf16), pltpu.VMEM((De, D), bf16)],
    )
    return pl.pallas_call(
        _experts_kernel,
        grid_spec=grid_spec,
        out_shape=jax.ShapeDtypeStruct((P, D), f32),
        compiler_params=_params(("arbitrary",), 52),
        name="experts",
    )(blk_e, n_used, xs, w1, w3, w2)


def _final_kernel(dcur_ref, dnxt_ref, x1_ref, p_ref, info_ref, ys_ref, wup_ref, wg_ref, gple_ref, gfin_ref, out_ref,
                  buf_ref, x2_ref, sems, *, final_norm):
    tm = x1_ref.shape[0]
    i = pl.program_id(0)
    slot = lax.rem(i, 2)

    def gather(dref, r, s):
        return [_row_copy(ys_ref, dref[0, k, r], buf_ref.at[s, k], r, sems.at[s]) for k in range(TOP_K)]

    def drain(s):
        def body(r, carry):
            for cp in gather(dcur_ref, 0, s):
                cp.wait()
            return carry
        lax.fori_loop(0, tm, body, 0, unroll=8)

    @pl.when(i == 0)
    def _():
        def body(r, carry):
            for cp in gather(dcur_ref, r, 0):
                cp.start()
            return carry
        lax.fori_loop(0, tm, body, 0, unroll=8)

    drain(slot)
    info = info_ref[...]
    x2_ref[...] = x1_ref[...] + buf_ref[slot, 0] * info[:, 2:3] + buf_ref[slot, 1] * info[:, 3:4]

    for r in range(tm):
        for cp in gather(dnxt_ref, r, 1 - slot):
            cp.start()

    ple = jnp.dot(p_ref[...].astype(bf16), wup_ref[...], preferred_element_type=f32)
    x2 = x2_ref[...]
    hn3 = _rms(x2, gple_ref[...]).astype(bf16)
    gate = jax.nn.sigmoid(jnp.dot(hn3, wg_ref[...], preferred_element_type=f32))
    x3 = x2 + gate * ple
    out_ref[...] = _rms(x3, gfin_ref[...]) if final_norm else x3

    @pl.when(i == pl.num_programs(0) - 1)
    def _():
        drain(1 - slot)


def _final(dest_blocks, x1, p2d, info, ys, w_up, w_gate, g_ple, g_final, *, final_norm):
    T, D = x1.shape
    PD = p2d.shape[1]
    tm = dest_blocks.shape[2]
    nt = T // tm
    row = lambda i: (i, 0)
    return pl.pallas_call(
        functools.partial(_final_kernel, final_norm=final_norm),
        grid=(nt,),
        in_specs=[pl.BlockSpec((1, TOP_K, tm), lambda i: (i, 0, 0), memory_space=pltpu.SMEM),
                  pl.BlockSpec((1, TOP_K, tm), lambda i: (jnp.minimum(i + 1, nt - 1), 0, 0),
                               memory_space=pltpu.SMEM),
                  pl.BlockSpec((tm, D), row),
                  pl.BlockSpec((tm, PD), row),
                  pl.BlockSpec((tm, LANES), row),
                  pl.BlockSpec(memory_space=pl.ANY),
                  _const_spec((PD, D)), _const_spec((D, D)), _const_spec((1, D)), _const_spec((1, D))],
        out_specs=pl.BlockSpec((tm, D), row),
        out_shape=jax.ShapeDtypeStruct((T, D), f32),
        scratch_shapes=[pltpu.VMEM((2, TOP_K, tm, D), f32), pltpu.VMEM((tm, D), f32),
                        pltpu.SemaphoreType.DMA((2,))],
        compiler_params=_params(("arbitrary",), 44),
        name="final",
    )(dest_blocks, dest_blocks, x1, p2d, info, ys, w_up, w_gate, g_ple, g_final)


def _layer(x2d, p2d, g_mix, w_in, conv_w, conv_b, b_gate, gn_m, ln_g, ln_b, w_s, b_s, w_bm, w_bg, w_out,
           g_ffn, w_rg, b_rg, w_re, b_re, w1, w3, w2, g_ple, w_ple_up, w_ple_gate, g_out, *, B, S, final_norm):
    T, D = x2d.shape
    H = b_gate.shape[0] // 2
    W = gn_m.shape[0]
    GW = ln_g.shape[0]
    NG = w_rg.shape[1]
    NE = w_re.shape[1]
    EPG = NE // NG
    BM = EXPERT_BLOCK
    row2 = lambda v: v.reshape(1, -1).astype(f32)

    n_if = 2 * H
    c_if = 4 * W
    c_uv = c_if + n_if
    w_a = w_in[:, :c_if].astype(bf16)
    w_b = w_in[:, c_uv:].astype(bf16)
    w_if = jnp.pad(w_in[:, c_if:c_uv], ((0, 0), (0, LANES - n_if))).astype(bf16)
    u_col = 4 * W
    gate_col = u_col + 2 * GW
    bg_pad = jnp.concatenate([b_gate.astype(f32), jnp.zeros((LANES - n_if,), f32)]).reshape(1, LANES)

    proj, pif = _inproj(x2d, row2(g_mix), w_a, w_b, w_if)
    h_m = _mlstm(proj, pif, conv_w.astype(f32), row2(conv_b), bg_pad, row2(gn_m), B=B, S=S, H=H, W=W)
    h_g = _gmlp(proj, row2(ln_g), row2(ln_b), w_s.astype(f32), jnp.transpose(b_s).astype(f32),
                T=T, GW=GW, u_col=u_col)

    w_r = jnp.concatenate([w_rg, w_re, jnp.zeros((D, LANES - NG - NE), w_rg.dtype)], axis=1).astype(bf16)
    b_r = jnp.concatenate([b_rg.astype(f32), b_re.astype(f32), jnp.zeros((LANES - NG - NE,), f32)]).reshape(1, LANES)
    x1, hn2, logits = _merge(h_m, h_g, proj, x2d, w_bm.astype(bf16), w_bg.astype(bf16), w_out.astype(bf16),
                             row2(g_ffn), w_r, b_r, gate_col=gate_col)

    info, cnt = _route(logits, NG=NG, EPG=EPG)
    counts = cnt[0, NG:NG + NE].astype(jnp.int32)
    padded = (counts + BM - 1) // BM * BM
    pends = jnp.cumsum(padded)
    pstarts = pends - padded
    NB = (T * TOP_K) // BM + NE
    blk_start = jnp.arange(NB, dtype=jnp.int32) * BM
    blk_e = jnp.minimum(jnp.sum((pends[None, :] <= blk_start[:, None]).astype(jnp.int32), axis=1), NE - 1)
    n_used = (pends[-1:] // BM).astype(jnp.int32)
    ps_row = jnp.zeros((1, LANES), f32).at[0, :NE].set(pstarts.astype(f32))
    seg = jnp.stack([jnp.concatenate([pstarts + counts, pends[-1:]]),
                     jnp.concatenate([pends, jnp.full((1,), NB * BM, jnp.int32)])]).astype(jnp.int32)

    dest = _dest(info, ps_row)
    tmd = min(256, T)
    dest_blocks = dest[:, :TOP_K].reshape(T // tmd, tmd, TOP_K).transpose(0, 2, 1)

    xs = _dispatch(seg, dest_blocks, hn2, NB * BM)
    ys = _experts(blk_e, n_used, xs, w1, w3, w2)
    return _final(dest_blocks, x1, p2d, info, ys, w_ple_up.astype(bf16), w_ple_gate.astype(bf16),
                  row2(g_ple), row2(g_out), final_norm=final_norm)


def kernel(x, p, g_mix, w_in, conv_w, conv_b, b_gate, gn_m, ln_g, ln_b, w_s, b_s, w_bm, w_bg, w_out, g_ffn, w_rg, b_rg, w_re, b_re, w1, w3, w2, g_ple, w_ple_up, w_ple_gate, g_final):
    B, S, D = x.shape
    depth = w_in.shape[0]
    x2d = x.reshape(B * S, D)
    for i in range(depth):
        last = i == depth - 1
        x2d = _layer(x2d, p[i].reshape(B * S, -1), g_mix[i], w_in[i], conv_w[i], conv_b[i], b_gate[i], gn_m[i],
                     ln_g[i], ln_b[i], w_s[i], b_s[i], w_bm[i], w_bg[i], w_out[i], g_ffn[i], w_rg[i], b_rg[i],
                     w_re[i], b_re[i], w1[i], w3[i], w2[i], g_ple[i], w_ple_up[i], w_ple_gate[i], g_final, B=B, S=S, final_norm=last)
    return x2d.reshape(B, S, D)
```

```python
import functools

import jax
import jax.numpy as jnp
from jax import lax
from jax.experimental import pallas as pl
from jax.experimental.pallas import tpu as pltpu

EPS = 1e-6
M_CHUNK = 128
TOP_K = 2
LANES = 128
SUBLANES = 8
MXU_DIM = 256
EXPERT_BLOCK = 256
PROJ_ROWS = 1024
PROJ_COLS = 1024
MIB = 1024 * 1024

f32 = jnp.float32
bf16 = jnp.bfloat16


def _params(semantics, vmem_mib):
    return pltpu.CompilerParams(dimension_semantics=semantics, vmem_limit_bytes=vmem_mib * MIB)


def _const_spec(shape):
    nd = len(shape)
    return pl.BlockSpec(shape, lambda *_: (0,) * nd, pipeline_mode=pl.Buffered(1))


def _rms(x, g):
    ms = jnp.mean(x * x, axis=-1, keepdims=True)
    return x * lax.rsqrt(ms + EPS) * g


def _norm_kernel(x_ref, g_ref, o_ref):
    o_ref[...] = _rms(x_ref[...], g_ref[...]).astype(o_ref.dtype)


def _norm(x2d, g):
    T, D = x2d.shape
    tm = min(512, T)
    return pl.pallas_call(
        _norm_kernel,
        grid=(T // tm,),
        in_specs=[pl.BlockSpec((tm, D), lambda i: (i, 0)), pl.BlockSpec((1, D), lambda i: (0, 0))],
        out_specs=pl.BlockSpec((tm, D), lambda i: (i, 0)),
        out_shape=jax.ShapeDtypeStruct((T, D), bf16),
        compiler_params=_params(("parallel",), 32),
        name="norm",
    )(x2d, g)


def _chunk_cols(n):
    return slice(n * MXU_DIM, (n + 1) * MXU_DIM)


def _dot_chunk(hn_ref, w_ref, n):
    return jnp.dot(hn_ref[...], w_ref[:, _chunk_cols(n)], preferred_element_type=f32)


def _proj_act_kernel(hn_ref, w_ref, o_ref, *, act):
    for n in range(o_ref.shape[1] // MXU_DIM):
        o_ref[:, _chunk_cols(n)] = act(_dot_chunk(hn_ref, w_ref, n)).astype(o_ref.dtype)


def _proj_v_kernel(hn_ref, w_ref, wif_ref, o_ref, oif_ref):
    _proj_act_kernel(hn_ref, w_ref, o_ref, act=lambda t: t)
    oif_ref[...] = jnp.dot(hn_ref[...], wif_ref[...], preferred_element_type=f32)


def _proj_qk_kernel(hn_ref, w_ref, cw_ref, cb_ref, o_ref, carry_ref, *, tiles_per_seq, KW, k_scale):
    i = pl.program_id(0)
    j = pl.program_id(1)
    tm = hn_ref.shape[0]
    scale = jnp.where(j == 1, k_scale, 1.0).astype(f32)

    @pl.when(lax.rem(i, tiles_per_seq) == 0)
    def _():
        carry_ref[j] = jnp.zeros(carry_ref.shape[1:], f32)

    for n in range(o_ref.shape[1] // MXU_DIM):
        cols = _chunk_cols(n)
        acc = _dot_chunk(hn_ref, w_ref, n)
        prev = carry_ref[j, :, cols]
        carry_ref[j, :, cols] = acc[tm - SUBLANES:, :]
        ext = jnp.concatenate([prev, acc], axis=0)
        y = cb_ref[:, cols] + cw_ref[KW - 1:KW, cols] * acc
        for s in range(1, KW):
            y = y + cw_ref[KW - 1 - s:KW - s, cols] * pltpu.roll(ext, s, axis=0)[SUBLANES:, :]
        o_ref[:, cols] = (y * jax.nn.sigmoid(y) * scale).astype(o_ref.dtype)


def _proj_vn_kernel(hn_ref, w_ref, lng_ref, lnb_ref, o_ref, g_ref):
    nch = o_ref.shape[1] // MXU_DIM
    width = o_ref.shape[1]
    total = 0.0
    for n in range(nch):
        g = jax.nn.gelu(_dot_chunk(hn_ref, w_ref, n))
        g_ref[:, _chunk_cols(n)] = g
        total = total + jnp.sum(g, axis=-1, keepdims=True)
    mu = total / width
    sq = 0.0
    for n in range(nch):
        xc = g_ref[:, _chunk_cols(n)] - mu
        sq = sq + jnp.sum(xc * xc, axis=-1, keepdims=True)
    inv = lax.rsqrt(sq / width + EPS)
    for n in range(nch):
        cols = _chunk_cols(n)
        o_ref[:, cols] = ((g_ref[:, cols] - mu) * inv * lng_ref[:, cols] + lnb_ref[:, cols]).astype(o_ref.dtype)


def _proj(name, body, hn, w, col0, ncol, *, extra=(), extra_specs=(), extra_out=(), extra_out_specs=(), scratch=(),
          tm):
    T, D = hn.shape
    tn = PROJ_COLS
    outs = pl.pallas_call(
        body,
        grid=(T // tm, ncol),
        in_specs=[pl.BlockSpec((tm, D), lambda i, j: (i, 0)),
                  pl.BlockSpec((D, tn), lambda i, j: (0, col0 + j))] + list(extra_specs),
        out_specs=[pl.BlockSpec((tm, tn), lambda i, j: (i, j))] + list(extra_out_specs),
        out_shape=[jax.ShapeDtypeStruct((T, ncol * tn), bf16)] + list(extra_out),
        scratch_shapes=list(scratch),
        compiler_params=_params(("arbitrary", "arbitrary"), 48),
        name=name,
    )(hn, w, *extra)
    return outs if extra_out else outs[0]


def _mlstm_kernel(q_ref, k_ref, v_ref, so_ref, if_ref, bg_ref, gn_ref, out_ref, c_ref, n_ref, m_ref, *, H, Dh):
    @pl.when(pl.program_id(1) == 0)
    def _():
        c_ref[...] = jnp.zeros_like(c_ref)
        n_ref[...] = jnp.zeros_like(n_ref)
        m_ref[...] = jnp.zeros_like(m_ref)

    for bb in range(q_ref.shape[0]):
        _mlstm_chunk(q_ref.at[bb], k_ref.at[bb], v_ref.at[bb], so_ref.at[bb], if_ref.at[bb], bg_ref, gn_ref,
                     out_ref.at[bb], c_ref.at[bb], n_ref.at[bb], m_ref.at[bb], H=H, Dh=Dh)


def _mlstm_chunk(q_ref, k_ref, v_ref, so_ref, if_ref, bg_ref, gn_ref, out_ref, c_ref, n_ref, m_ref, *, H, Dh):
    L = M_CHUNK
    gp = if_ref[...] + bg_ref[...]
    lane = lax.broadcasted_iota(jnp.int32, (L, LANES), 1)
    lf = jnp.minimum(gp, 0.0) - jnp.log1p(jnp.exp(-jnp.abs(gp)))
    z = jnp.where(lane < H, gp, lf)
    row = lax.broadcasted_iota(jnp.int32, (L, L), 0)
    col = lax.broadcasted_iota(jnp.int32, (L, L), 1)
    causal = row >= col
    bc = jnp.dot(causal.astype(f32), z, precision=lax.Precision.HIGHEST, preferred_element_type=f32)
    zt = z.T
    bct = bc.T

    for h in range(H):
        sl = slice(h * Dh, (h + 1) * Dh)
        qb = q_ref[:, sl]
        kb = k_ref[:, sl]
        vb = v_ref[:, sl]
        b_col = bc[:, H + h:H + h + 1]
        ig_col = z[:, h:h + 1]
        b_row = bct[H + h:H + h + 1, :]
        ig_row = zt[h:h + 1, :]
        g = b_col[L - 1:L, :]
        m_prev = m_ref[h, 0:1, 0:1]
        n_prev = n_ref[h, 0:1, :]

        a_col = g - b_col + ig_col
        m_loc = jnp.max(a_col, axis=0, keepdims=True)
        w_loc = jnp.exp(a_col - m_loc)

        log_d = jnp.where(causal, b_col - b_row + ig_row, -jnp.inf)
        log_inter = b_col + m_prev
        mt = jnp.maximum(log_inter, jnp.max(log_d, axis=-1, keepdims=True))
        d = jnp.exp(log_d - mt)
        s_inter = jnp.exp(log_inter - mt)

        s = lax.dot_general(qb, kb, (((1,), (1,)), ((), ())), preferred_element_type=f32) * d
        num = (jnp.dot(s.astype(bf16), vb, preferred_element_type=f32)
               + s_inter * jnp.dot(qb, c_ref[h].astype(bf16), preferred_element_type=f32))
        den = (jnp.sum(s, axis=-1, keepdims=True)
               + s_inter * jnp.sum(qb.astype(f32) * n_prev, axis=-1, keepdims=True))
        hh = num / jnp.maximum(jnp.abs(den), jnp.exp(-mt))

        m_new = jnp.maximum(g + m_prev, m_loc)
        s_old = jnp.exp(g + m_prev - m_new)
        s_loc = jnp.exp(m_loc - m_new)
        kw = kb.astype(f32) * w_loc
        kv = jnp.dot(kw.T.astype(bf16), vb, preferred_element_type=f32)
        c_ref[h] = s_old * c_ref[h] + s_loc * kv
        n_ref[h, 0:1, :] = s_old * n_prev + s_loc * jnp.sum(kw, axis=0, keepdims=True)
        m_ref[h] = jnp.broadcast_to(m_new, m_ref.shape[1:])

        mu = jnp.mean(hh, axis=-1, keepdims=True)
        xc = hh - mu
        var = jnp.mean(xc * xc, axis=-1, keepdims=True)
        hn = xc * lax.rsqrt(var + EPS) * gn_ref[:, sl]
        out_ref[:, sl] = (so_ref[:, sl].astype(f32) * hn).astype(out_ref.dtype)


def _mlstm(qk, v, so, pif, bg_pad, gn_m, *, B, S, H):
    T, W = v.shape
    L = M_CHUNK
    Dh = W // H
    BB = 1
    seq = lambda a: a.reshape(B, S, a.shape[-1])
    blk = lambda width, cb: pl.BlockSpec((BB, L, width), lambda b, c: (b, c, cb))
    out = pl.pallas_call(
        functools.partial(_mlstm_kernel, H=H, Dh=Dh),
        grid=(B // BB, S // L),
        in_specs=[blk(W, 0), blk(W, 1), blk(W, 0), blk(W, 0), blk(LANES, 0),
                  pl.BlockSpec((1, LANES), lambda b, c: (0, 0)),
                  pl.BlockSpec((1, W), lambda b, c: (0, 0))],
        out_specs=blk(W, 0),
        out_shape=jax.ShapeDtypeStruct((B, S, W), bf16),
        scratch_shapes=[pltpu.VMEM((BB, H, Dh, Dh), f32),
                        pltpu.VMEM((BB, H, SUBLANES, Dh), f32),
                        pltpu.VMEM((BB, H, SUBLANES, LANES), f32)],
        compiler_params=_params(("parallel", "arbitrary"), 32),
        name="mlstm",
    )(seq(qk), seq(qk), seq(v), seq(so), seq(pif), bg_pad, gn_m)
    return out.reshape(T, W)


def _merge_kernel(hm_ref, ug_ref, vn_ref, gm_ref, gg_ref, x_ref, ws_ref, bst_ref, wbm_ref, wbg_ref, wout_ref,
                  gffn_ref, wr_ref, br_ref, x1_ref, hn2_ref, lg_ref, hg_ref):
    tm = x_ref.shape[0]
    G, L, _ = ws_ref.shape
    Gd = ug_ref.shape[1] // G
    for c in range(tm // L):
        rows = slice(c * L, (c + 1) * L)
        for g in range(G):
            cols = slice(g * Gd, (g + 1) * Gd)
            mixed = jnp.dot(ws_ref[g], vn_ref[rows, cols], preferred_element_type=f32) + bst_ref[:, g:g + 1]
            hg_ref[rows, cols] = (ug_ref[rows, cols].astype(f32) * mixed).astype(hg_ref.dtype)

    a = jnp.dot(hm_ref[...], wbm_ref[...], preferred_element_type=f32)
    b = jnp.dot(hg_ref[...], wbg_ref[...], preferred_element_type=f32)
    merged = gm_ref[...].astype(f32) * a + gg_ref[...].astype(f32) * b
    x1 = x_ref[...] + jnp.dot(merged.astype(bf16), wout_ref[...], preferred_element_type=f32)
    x1_ref[...] = x1
    hn2 = _rms(x1, gffn_ref[...])
    hn2_ref[...] = hn2
    lg_ref[...] = jnp.dot(hn2.astype(bf16), wr_ref[...], preferred_element_type=f32) + br_ref[...]


def _merge(h_m, ug, vn, gates, x2d, ws_tril, b_st, w_bm, w_bg, w_out, g_ffn, w_r, b_r):
    T, D = x2d.shape
    W = h_m.shape[1]
    GW = ug.shape[1]
    G, L, _ = ws_tril.shape
    tm = min(256, T)
    row = lambda i: (i, 0)
    return pl.pallas_call(
        _merge_kernel,
        grid=(T // tm,),
        in_specs=[pl.BlockSpec((tm, W), row),
                  pl.BlockSpec((tm, GW), row),
                  pl.BlockSpec((tm, GW), row),
                  pl.BlockSpec((tm, D), lambda i: (i, 0)),
                  pl.BlockSpec((tm, D), lambda i: (i, 1)),
                  pl.BlockSpec((tm, D), row),
                  _const_spec((G, L, L)), _const_spec((L, G)),
                  _const_spec((W, D)), _const_spec((GW, D)), _const_spec((D, D)),
                  _const_spec((1, D)), _const_spec((D, LANES)), _const_spec((1, LANES))],
        out_specs=[pl.BlockSpec((tm, D), row), pl.BlockSpec((tm, D), row), pl.BlockSpec((tm, LANES), row)],
        out_shape=[jax.ShapeDtypeStruct((T, D), f32), jax.ShapeDtypeStruct((T, D), f32),
                   jax.ShapeDtypeStruct((T, LANES), f32)],
        scratch_shapes=[pltpu.VMEM((tm, GW), bf16)],
        compiler_params=_params(("parallel",), 48),
        name="merge",
    )(h_m, ug, vn, gates, gates, x2d, ws_tril, b_st, w_bm, w_bg, w_out, g_ffn, w_r, b_r)


def _route_kernel(lg_ref, info_ref, cnt_ref, carry_ref, *, NG, EPG):
    tm = lg_ref.shape[0]

    @pl.when(pl.program_id(0) == 0)
    def _():
        carry_ref[...] = jnp.zeros_like(carry_ref)

    lg = lg_ref[...]
    lane = lax.broadcasted_iota(jnp.int32, (tm, LANES), 1)
    lanef = lane.astype(f32)
    big = float(LANES)

    def first_max(vals):
        mx = jnp.max(vals, axis=-1, keepdims=True)
        idx = jnp.min(jnp.where(vals == mx, lanef, big), axis=-1, keepdims=True)
        return mx, idx

    is_g = lane < NG
    gmax, grp = first_max(jnp.where(is_g, lg, -jnp.inf))
    p_grp = 1.0 / jnp.sum(jnp.where(is_g, jnp.exp(lg - gmax), 0.0), axis=-1, keepdims=True)
    lo = NG + grp * EPG
    el = jnp.where(jnp.logical_and(lanef >= lo, lanef < lo + EPG), lg, -jnp.inf)
    v1, i1 = first_max(el)
    v2, i2 = first_max(jnp.where(lanef == i1, -jnp.inf, el))
    t = jnp.exp(v2 - v1)
    w1 = p_grp / (1.0 + t)
    w2 = p_grp * t / (1.0 + t)

    hit1 = lanef == i1
    hit2 = lanef == i2
    oh = jnp.where(jnp.logical_or(hit1, hit2), 1.0, 0.0)
    r = lax.broadcasted_iota(jnp.int32, (tm, tm), 0)
    c = lax.broadcasted_iota(jnp.int32, (tm, tm), 1)
    before = jnp.where(r > c, 1.0, 0.0).astype(bf16)
    carry = carry_ref[0:1, :]
    earlier = jnp.dot(before, oh.astype(bf16), preferred_element_type=f32) + carry
    rank1 = jnp.sum(jnp.where(hit1, earlier, 0.0), axis=-1, keepdims=True)
    rank2 = jnp.sum(jnp.where(hit2, earlier, 0.0), axis=-1, keepdims=True)
    new_carry = carry + jnp.sum(oh, axis=0, keepdims=True)
    carry_ref[...] = jnp.broadcast_to(new_carry, carry_ref.shape)
    cnt_ref[...] = jnp.broadcast_to(new_carry, cnt_ref.shape)

    vals = (i1 - NG, i2 - NG, w1, w2, rank1, rank2)
    info = jnp.zeros((tm, LANES), f32)
    for k, v in enumerate(vals):
        info = jnp.where(lane == k, v, info)
    info_ref[...] = info


def _route(logits, *, NG, EPG):
    T = logits.shape[0]
    tm = min(512, T)
    return pl.pallas_call(
        functools.partial(_route_kernel, NG=NG, EPG=EPG),
        grid=(T // tm,),
        in_specs=[pl.BlockSpec((tm, LANES), lambda i: (i, 0))],
        out_specs=[pl.BlockSpec((tm, LANES), lambda i: (i, 0)),
                   pl.BlockSpec((SUBLANES, LANES), lambda i: (0, 0))],
        out_shape=[jax.ShapeDtypeStruct((T, LANES), f32), jax.ShapeDtypeStruct((SUBLANES, LANES), f32)],
        scratch_shapes=[pltpu.VMEM((SUBLANES, LANES), f32)],
        compiler_params=_params(("arbitrary",), 32),
        name="route",
    )(logits)


def _dest_kernel(info_ref, ps_ref, dest_ref):
    tm = info_ref.shape[0]
    info = info_ref[...]
    lane = lax.broadcasted_iota(jnp.int32, (tm, LANES), 1)
    lanef = lane.astype(f32)
    ps = ps_ref[...]
    d1 = jnp.sum(jnp.where(lanef == info[:, 0:1], ps, 0.0), axis=-1, keepdims=True) + info[:, 4:5]
    d2 = jnp.sum(jnp.where(lanef == info[:, 1:2], ps, 0.0), axis=-1, keepdims=True) + info[:, 5:6]
    dest_ref[...] = jnp.where(lane == 0, d1, jnp.where(lane == 1, d2, 0.0)).astype(jnp.int32)


def _dest(info, pstart_row):
    T = info.shape[0]
    tm = min(512, T)
    return pl.pallas_call(
        _dest_kernel,
        grid=(T // tm,),
        in_specs=[pl.BlockSpec((tm, LANES), lambda i: (i, 0)), pl.BlockSpec((1, LANES), lambda i: (0, 0))],
        out_specs=pl.BlockSpec((tm, LANES), lambda i: (i, 0)),
        out_shape=jax.ShapeDtypeStruct((T, LANES), jnp.int32),
        compiler_params=_params(("parallel",), 32),
        name="dest",
    )(info, pstart_row)


def _row_copy(src, si, dst, di, sem, n=1):
    return pltpu.make_async_copy(src.at[pl.ds(si, n)], dst.at[pl.ds(di, n)], sem)


def _fill_rows(seg_ref, zero_ref, xs_ref, sem, *, wait):
    zrows = zero_ref.shape[0]

    def act(start, n):
        cp = _row_copy(zero_ref, 0, xs_ref, start, sem, n)
        cp.wait() if wait else cp.start()

    def aligned(start, n):
        act(pl.multiple_of(start, SUBLANES), n)

    def per_segment(e, carry):
        lo = seg_ref[0, e]
        hi = seg_ref[1, e]
        lo_al = jnp.minimum((lo + SUBLANES - 1) // SUBLANES * SUBLANES, hi)

        def single(r, cc):
            act(r, 1)
            return cc

        lax.fori_loop(lo, lo_al, single, 0)
        count = hi - lo_al
        nfull = count // zrows

        def full(c, cc):
            aligned(lo_al + c * zrows, zrows)
            return cc

        lax.fori_loop(0, nfull, full, 0)
        pos = lo_al + nfull * zrows
        rem = count - nfull * zrows
        n = zrows // 2
        while n >= SUBLANES:
            take = (rem & n) != 0
            pl.when(take)(functools.partial(aligned, pos, n))
            pos = pos + jnp.where(take, n, 0)
            n //= 2
        return carry

    lax.fori_loop(0, seg_ref.shape[1], per_segment, 0)


def _dispatch_kernel(seg_ref, dest_ref, hn_ref, xs_ref, zero_ref, sem, zsem):
    tm = hn_ref.shape[0]

    @pl.when(pl.program_id(0) == 0)
    def _():
        zero_ref[...] = jnp.zeros_like(zero_ref)
        _fill_rows(seg_ref, zero_ref, xs_ref, zsem, wait=False)
        _fill_rows(seg_ref, zero_ref, xs_ref, zsem, wait=True)

    def issue(r, carry):
        for k in range(TOP_K):
            _row_copy(hn_ref, r, xs_ref, dest_ref[0, k, r], sem).start()
        return carry

    lax.fori_loop(0, tm, issue, 0, unroll=8)

    def drain(r, carry):
        for k in range(TOP_K):
            _row_copy(hn_ref, 0, xs_ref, 0, sem).wait()
        return carry

    lax.fori_loop(0, tm, drain, 0, unroll=8)


def _dispatch(seg, dest_blocks, hn2, n_rows):
    T, D = hn2.shape
    tm = dest_blocks.shape[2]
    grid_spec = pltpu.PrefetchScalarGridSpec(
        num_scalar_prefetch=1,
        grid=(T // tm,),
        in_specs=[pl.BlockSpec((1, TOP_K, tm), lambda i, seg: (i, 0, 0), memory_space=pltpu.SMEM),
                  pl.BlockSpec((tm, D), lambda i, seg: (i, 0))],
        out_specs=pl.BlockSpec(memory_space=pl.ANY),
        scratch_shapes=[pltpu.VMEM((EXPERT_BLOCK // 2, D), hn2.dtype), pltpu.SemaphoreType.DMA(()),
                        pltpu.SemaphoreType.DMA(())],
    )
    return pl.pallas_call(
        _dispatch_kernel,
        grid_spec=grid_spec,
        out_shape=jax.ShapeDtypeStruct((n_rows, D), hn2.dtype),
        compiler_params=_params(("arbitrary",), 32),
        name="dispatch",
    )(seg, dest_blocks, hn2)


def _experts_kernel(be_ref, nu_ref, nx_ref, xs_ref, w1_ref, w3_ref, w2_ref, ys_ref, st1, st3, st2, wb1, wb3, wb2, sems):
    i = pl.program_id(0)
    e = be_ref[i]
    prev = be_ref[jnp.maximum(i - 1, 0)]

    def fetch(ex):
        return [pltpu.make_async_copy(w1_ref.at[ex], st1, sems.at[0]),
                pltpu.make_async_copy(w3_ref.at[ex], st3, sems.at[1]),
                pltpu.make_async_copy(w2_ref.at[ex], st2, sems.at[2])]

    @pl.when(i == 0)
    def _():
        for cp in fetch(e):
            cp.start()

    @pl.when(jnp.logical_or(i == 0, e != prev))
    def _():
        for cp in fetch(e):
            cp.wait()
        wb1[...] = st1[...].astype(bf16)
        wb3[...] = st3[...].astype(bf16)
        wb2[...] = st2[...].astype(bf16)
        nxt = nx_ref[i]

        @pl.when(nxt >= 0)
        def _():
            for cp in fetch(nxt):
                cp.start()

    @pl.when(i < nu_ref[0])
    def _():
        x = xs_ref[...].astype(bf16)
        h1 = jnp.dot(x, wb1[...], preferred_element_type=f32)
        h3 = jnp.dot(x, wb3[...], preferred_element_type=f32)
        a = (h1 * jax.nn.sigmoid(h1) * h3).astype(bf16)
        ys_ref[...] = jnp.dot(a, wb2[...], preferred_element_type=f32)

    @pl.when(i >= nu_ref[0])
    def _():
        ys_ref[...] = jnp.zeros_like(ys_ref)


def _experts(blk_e, n_used, blk_next, xs, w1, w3, w2):
    P, D = xs.shape
    NE, _, De = w1.shape
    BM = EXPERT_BLOCK
    grid_spec = pltpu.PrefetchScalarGridSpec(
        num_scalar_prefetch=3,
        grid=(P // BM,),
        in_specs=[pl.BlockSpec((BM, D), lambda i, be, nu, nx: (jnp.minimum(i, nu[0] - 1), 0)),
                  pl.BlockSpec(memory_space=pl.ANY), pl.BlockSpec(memory_space=pl.ANY),
                  pl.BlockSpec(memory_space=pl.ANY)],
        out_specs=pl.BlockSpec((BM, D), lambda i, be, nu, nx: (i, 0)),
        scratch_shapes=[pltpu.VMEM((D, De), w1.dtype), pltpu.VMEM((D, De), w3.dtype), pltpu.VMEM((De, D), w2.dtype),
                        pltpu.VMEM((D, De), bf16), pltpu.VMEM((D, De), bf16), pltpu.VMEM((De, D), bf16),
                        pltpu.SemaphoreType.DMA((3,))],
    )
    return pl.pallas_call(
        _experts_kernel,
        grid_spec=grid_spec,
        out_shape=jax.ShapeDtypeStruct((P, D), f32),
        compiler_params=_params(("arbitrary",), 48),
        name="experts",
    )(blk_e, n_used, blk_next, xs, w1, w3, w2)


def _final_kernel(dcur_ref, dnxt_ref, x1_ref, p_ref, info_ref, ys_ref, wup_ref, wg_ref, gple_ref, gfin_ref, out_ref,
                  buf_ref, x2_ref, sems, *, final_norm):
    tm = x1_ref.shape[0]
    i = pl.program_id(0)
    slot = lax.rem(i, 2)

    def gather(dref, r, s):
        return [_row_copy(ys_ref, dref[0, k, r], buf_ref.at[s, k], r, sems.at[s]) for k in range(TOP_K)]

    def drain(s):
        def body(r, carry):
            for cp in gather(dcur_ref, 0, s):
                cp.wait()
            return carry
        lax.fori_loop(0, tm, body, 0, unroll=8)

    @pl.when(i == 0)
    def _():
        def body(r, carry):
            for cp in gather(dcur_ref, r, 0):
                cp.start()
            return carry
        lax.fori_loop(0, tm, body, 0, unroll=8)

    drain(slot)
    info = info_ref[...]
    x2_ref[...] = x1_ref[...] + buf_ref[slot, 0] * info[:, 2:3] + buf_ref[slot, 1] * info[:, 3:4]

    for r in range(tm):
        for cp in gather(dnxt_ref, r, 1 - slot):
            cp.start()

    ple = jnp.dot(p_ref[...].astype(bf16), wup_ref[...], preferred_element_type=f32)
    x2 = x2_ref[...]
    hn3 = _rms(x2, gple_ref[...]).astype(bf16)
    gate = jax.nn.sigmoid(jnp.dot(hn3, wg_ref[...], preferred_element_type=f32))
    x3 = x2 + gate * ple
    out_ref[...] = _rms(x3, gfin_ref[...]) if final_norm else x3

    @pl.when(i == pl.num_programs(0) - 1)
    def _():
        drain(1 - slot)


def _final(dest_blocks, x1, p2d, info, ys, w_up, w_gate, g_ple, g_final, *, final_norm):
    T, D = x1.shape
    PD = p2d.shape[1]
    tm = dest_blocks.shape[2]
    nt = T // tm
    row = lambda i: (i, 0)
    return pl.pallas_call(
        functools.partial(_final_kernel, final_norm=final_norm),
        grid=(nt,),
        in_specs=[pl.BlockSpec((1, TOP_K, tm), lambda i: (i, 0, 0), memory_space=pltpu.SMEM),
                  pl.BlockSpec((1, TOP_K, tm), lambda i: (jnp.minimum(i + 1, nt - 1), 0, 0),
                               memory_space=pltpu.SMEM),
                  pl.BlockSpec((tm, D), row),
                  pl.BlockSpec((tm, PD), row),
                  pl.BlockSpec((tm, LANES), row),
                  pl.BlockSpec(memory_space=pl.ANY),
                  _const_spec((PD, D)), _const_spec((D, D)), _const_spec((1, D)), _const_spec((1, D))],
        out_specs=pl.BlockSpec((tm, D), row),
        out_shape=jax.ShapeDtypeStruct((T, D), f32),
        scratch_shapes=[pltpu.VMEM((2, TOP_K, tm, D), ys.dtype), pltpu.VMEM((tm, D), f32),
                        pltpu.SemaphoreType.DMA((2,))],
        compiler_params=_params(("arbitrary",), 44),
        name="final",
    )(dest_blocks, dest_blocks, x1, p2d, info, ys, w_up, w_gate, g_ple, g_final)


def _layer(x2d, p2d, g_mix, w_in, conv_w, conv_b, b_gate, gn_m, ln_g, ln_b, w_s, b_s, w_bm, w_bg, w_out,
           g_ffn, w_rg, b_rg, w_re, b_re, w1, w3, w2, g_ple, w_ple_up, w_ple_gate, g_out, *, B, S, final_norm):
    T, D = x2d.shape
    H = b_gate.shape[0] // 2
    W = gn_m.shape[0]
    GW = ln_g.shape[0]
    NG = w_rg.shape[1]
    NE = w_re.shape[1]
    EPG = NE // NG
    BM = EXPERT_BLOCK
    KW = conv_w.shape[0]
    tn = PROJ_COLS
    row2 = lambda v: v.reshape(1, -1).astype(f32)

    n_if = 2 * H
    c_if = 4 * W
    c_uv = c_if + n_if
    w_bf = w_in.astype(bf16)
    w_tail = w_bf[:, c_uv:]
    w_if = jnp.pad(w_bf[:, c_if:c_uv], ((0, 0), (0, LANES - n_if)))
    bg_pad = jnp.concatenate([b_gate.astype(f32), jnp.zeros((LANES - n_if,), f32)]).reshape(1, LANES)

    tm = min(PROJ_ROWS, S)
    hn = _norm(x2d, row2(g_mix))
    hspec = lambda shape: pl.BlockSpec(shape, lambda i, j: (0, j))
    qk = _proj("proj_qk",
               functools.partial(_proj_qk_kernel, tiles_per_seq=S // tm, KW=KW, k_scale=(W // H) ** -0.5),
               hn, w_bf, 0, 2 * W // tn, tm=tm,
               extra=(conv_w.astype(f32), row2(conv_b)), extra_specs=(hspec((KW, tn)), hspec((1, tn))),
               scratch=(pltpu.VMEM((2 * W // tn, SUBLANES, tn), f32),))
    v, pif = _proj("proj_v", _proj_v_kernel, hn, w_bf, 2 * W // tn, W // tn, tm=tm,
                   extra=(w_if,), extra_specs=(pl.BlockSpec((D, LANES), lambda i, j: (0, 0)),),
                   extra_out=(jax.ShapeDtypeStruct((T, LANES), f32),),
                   extra_out_specs=(pl.BlockSpec((tm, LANES), lambda i, j: (i, 0)),))
    so = _proj("proj_o", functools.partial(_proj_act_kernel, act=jax.nn.sigmoid), hn, w_bf, 3 * W // tn, W // tn, tm=tm)
    ug = _proj("proj_u", functools.partial(_proj_act_kernel, act=jax.nn.gelu), hn, w_tail, 0, GW // tn, tm=tm)
    vn = _proj("proj_vn", _proj_vn_kernel, hn, w_tail, GW // tn, 1, tm=tm,
               extra=(row2(ln_g), row2(ln_b)), extra_specs=(hspec((1, tn)), hspec((1, tn))),
               scratch=(pltpu.VMEM((tm, tn), f32),))
    gates = _proj("proj_gates", functools.partial(_proj_act_kernel, act=jax.nn.sigmoid), hn, w_tail,
                  2 * GW // tn, 2 * D // tn, tm=tm)

    h_m = _mlstm(qk, v, so, pif, bg_pad, row2(gn_m), B=B, S=S, H=H)

    w_r = jnp.concatenate([w_rg, w_re, jnp.zeros((D, LANES - NG - NE), w_rg.dtype)], axis=1).astype(bf16)
    b_r = jnp.concatenate([b_rg.astype(f32), b_re.astype(f32), jnp.zeros((LANES - NG - NE,), f32)]).reshape(1, LANES)
    x1, hn2, logits = _merge(h_m, ug, vn, gates, x2d, jnp.tril(w_s).astype(bf16), jnp.transpose(b_s).astype(f32),
                             w_bm.astype(bf16), w_bg.astype(bf16), w_out.astype(bf16), row2(g_ffn), w_r, b_r)

    info, cnt = _route(logits, NG=NG, EPG=EPG)
    counts = cnt[0, NG:NG + NE].astype(jnp.int32)
    padded = (counts + BM - 1) // BM * BM
    pends = jnp.cumsum(padded)
    pstarts = pends - padded
    NB = (T * TOP_K) // BM + NE
    blk_start = jnp.arange(NB, dtype=jnp.int32) * BM
    blk_e = jnp.minimum(jnp.sum((pends[None, :] <= blk_start[:, None]).astype(jnp.int32), axis=1), NE - 1)
    n_used = (pends[-1:] // BM).astype(jnp.int32)
    run_end = pends[blk_e] // BM
    blk_next = jnp.where(run_end < n_used[0], blk_e[jnp.minimum(run_end, NB - 1)], -1).astype(jnp.int32)
    ps_row = jnp.zeros((1, LANES), f32).at[0, :NE].set(pstarts.astype(f32))
    seg = jnp.stack([jnp.concatenate([pstarts + counts, pends[-1:]]),
                     jnp.concatenate([pends, jnp.full((1,), NB * BM, jnp.int32)])]).astype(jnp.int32)

    dest = _dest(info, ps_row)
    tmd = min(256, T)
    dest_blocks = dest[:, :TOP_K].reshape(T // tmd, tmd, TOP_K).transpose(0, 2, 1)

    xs = _dispatch(seg, dest_blocks, hn2, NB * BM)
    ys = _experts(blk_e, n_used, blk_next, xs, w1, w3, w2)
    return _final(dest_blocks, x1, p2d, info, ys, w_ple_up.astype(bf16), w_ple_gate.astype(bf16),
                  row2(g_ple), row2(g_out), final_norm=final_norm)


def kernel(x, p, g_mix, w_in, conv_w, conv_b, b_gate, gn_m, ln_g, ln_b, w_s, b_s, w_bm, w_bg, w_out, g_ffn, w_rg, b_rg, w_re, b_re, w1, w3, w2, g_ple, w_ple_up, w_ple_gate, g_final):
    B, S, D = x.shape
    depth = w_in.shape[0]
    x2d = x.reshape(B * S, D)
    for i in range(depth):
        last = i == depth - 1
        x2d = _layer(x2d, p[i].reshape(B * S, -1), g_mix[i], w_in[i], conv_w[i], conv_b[i], b_gate[i], gn_m[i],
                     ln_g[i], ln_b[i], w_s[i], b_s[i], w_bm[i], w_bg[i], w_out[i], g_ffn[i], w_rg[i], b_rg[i],
                     w_re[i], b_re[i], w1[i], w3[i], w2[i], g_ple[i], w_ple_up[i], w_ple_gate[i], g_final,
                     B=B, S=S, final_norm=last)
    return x2d.reshape(B, S, D)
```

```python
import functools

import jax
import jax.numpy as jnp
from jax import lax
from jax.experimental import pallas as pl
from jax.experimental.pallas import tpu as pltpu

EPS = 1e-6
M_CHUNK = 128
TOP_K = 2
LANES = 128
SUBLANES = 8
MXU_DIM = 256
EXPERT_BLOCK = 256
PROJ_ROWS = 1024
PROJ_COLS = 1024
MIB = 1024 * 1024

f32 = jnp.float32
bf16 = jnp.bfloat16


def _params(semantics, vmem_mib):
    return pltpu.CompilerParams(dimension_semantics=semantics, vmem_limit_bytes=vmem_mib * MIB)


def _const_spec(shape):
    nd = len(shape)
    return pl.BlockSpec(shape, lambda *_: (0,) * nd, pipeline_mode=pl.Buffered(1))


def _rms(x, g):
    ms = jnp.mean(x * x, axis=-1, keepdims=True)
    return x * lax.rsqrt(ms + EPS) * g


def _chunk_cols(n):
    return slice(n * MXU_DIM, (n + 1) * MXU_DIM)


def _dot_chunk(hn_ref, w_ref, n):
    return jnp.dot(hn_ref[...], w_ref[:, _chunk_cols(n)], preferred_element_type=f32)


def _proj_act_kernel(hn_ref, w_ref, o_ref, *, act):
    for n in range(o_ref.shape[1] // MXU_DIM):
        o_ref[:, _chunk_cols(n)] = act(_dot_chunk(hn_ref, w_ref, n)).astype(o_ref.dtype)


def _proj_v_kernel(hn_ref, w_ref, wif_ref, o_ref, oif_ref):
    _proj_act_kernel(hn_ref, w_ref, o_ref, act=lambda t: t)
    oif_ref[...] = jnp.dot(hn_ref[...], wif_ref[...], preferred_element_type=f32)


def _proj_qk_kernel(x_ref, g_ref, w_ref, cw_ref, cb_ref, o_ref, hn_ref, carry_ref, *, tiles_per_seq, KW, k_scale):
    i = pl.program_id(0)
    j = pl.program_id(1)
    tm = hn_ref.shape[0]
    scale = jnp.where(j == 1, k_scale, 1.0).astype(f32)

    @pl.when(j == 0)
    def _():
        hn_ref[...] = _rms(x_ref[...], g_ref[...]).astype(hn_ref.dtype)

    @pl.when(lax.rem(i, tiles_per_seq) == 0)
    def _():
        carry_ref[j] = jnp.zeros(carry_ref.shape[1:], f32)

    for n in range(o_ref.shape[1] // MXU_DIM):
        cols = _chunk_cols(n)
        acc = _dot_chunk(hn_ref, w_ref, n)
        prev = carry_ref[j, :, cols]
        carry_ref[j, :, cols] = acc[tm - SUBLANES:, :]
        ext = jnp.concatenate([prev, acc], axis=0)
        y = cb_ref[:, cols] + cw_ref[KW - 1:KW, cols] * acc
        for s in range(1, KW):
            y = y + cw_ref[KW - 1 - s:KW - s, cols] * pltpu.roll(ext, s, axis=0)[SUBLANES:, :]
        o_ref[:, cols] = (y * jax.nn.sigmoid(y) * scale).astype(o_ref.dtype)


def _proj_vn_kernel(hn_ref, w_ref, lng_ref, lnb_ref, o_ref, g_ref):
    nch = o_ref.shape[1] // MXU_DIM
    width = o_ref.shape[1]
    total = 0.0
    for n in range(nch):
        g = jax.nn.gelu(_dot_chunk(hn_ref, w_ref, n))
        g_ref[:, _chunk_cols(n)] = g
        total = total + jnp.sum(g, axis=-1, keepdims=True)
    mu = total / width
    sq = 0.0
    for n in range(nch):
        xc = g_ref[:, _chunk_cols(n)] - mu
        sq = sq + jnp.sum(xc * xc, axis=-1, keepdims=True)
    inv = lax.rsqrt(sq / width + EPS)
    for n in range(nch):
        cols = _chunk_cols(n)
        o_ref[:, cols] = ((g_ref[:, cols] - mu) * inv * lng_ref[:, cols] + lnb_ref[:, cols]).astype(o_ref.dtype)


def _proj(name, body, hn, w, col0, ncol, *, extra=(), extra_specs=(), extra_out=(), extra_out_specs=(), scratch=(),
          tm):
    T, D = hn.shape
    tn = PROJ_COLS
    outs = pl.pallas_call(
        body,
        grid=(T // tm, ncol),
        in_specs=[pl.BlockSpec((tm, D), lambda i, j: (i, 0)),
                  pl.BlockSpec((D, tn), lambda i, j: (0, col0 + j))] + list(extra_specs),
        out_specs=[pl.BlockSpec((tm, tn), lambda i, j: (i, j))] + list(extra_out_specs),
        out_shape=[jax.ShapeDtypeStruct((T, ncol * tn), bf16)] + list(extra_out),
        scratch_shapes=list(scratch),
        compiler_params=_params(("arbitrary", "arbitrary"), 48),
        name=name,
    )(hn, w, *extra)
    return outs if extra_out else outs[0]


def _mlstm_kernel(q_ref, k_ref, v_ref, so_ref, if_ref, bg_ref, gn_ref, out_ref, c_ref, n_ref, m_ref, *, H, Dh):
    @pl.when(pl.program_id(1) == 0)
    def _():
        c_ref[...] = jnp.zeros_like(c_ref)
        n_ref[...] = jnp.zeros_like(n_ref)
        m_ref[...] = jnp.zeros_like(m_ref)

    for bb in range(q_ref.shape[0]):
        _mlstm_chunk(q_ref.at[bb], k_ref.at[bb], v_ref.at[bb], so_ref.at[bb], if_ref.at[bb], bg_ref, gn_ref,
                     out_ref.at[bb], c_ref.at[bb], n_ref.at[bb], m_ref.at[bb], H=H, Dh=Dh)


def _mlstm_chunk(q_ref, k_ref, v_ref, so_ref, if_ref, bg_ref, gn_ref, out_ref, c_ref, n_ref, m_ref, *, H, Dh):
    L = M_CHUNK
    gp = if_ref[...] + bg_ref[...]
    lane = lax.broadcasted_iota(jnp.int32, (L, LANES), 1)
    lf = jnp.minimum(gp, 0.0) - jnp.log1p(jnp.exp(-jnp.abs(gp)))
    z = jnp.where(lane < H, gp, lf)
    row = lax.broadcasted_iota(jnp.int32, (L, L), 0)
    col = lax.broadcasted_iota(jnp.int32, (L, L), 1)
    causal = row >= col
    bc = jnp.dot(causal.astype(f32), z, precision=lax.Precision.HIGHEST, preferred_element_type=f32)
    zt = z.T
    bct = bc.T

    for h in range(H):
        sl = slice(h * Dh, (h + 1) * Dh)
        qb = q_ref[:, sl]
        kb = k_ref[:, sl]
        vb = v_ref[:, sl]
        b_col = bc[:, H + h:H + h + 1]
        ig_col = z[:, h:h + 1]
        b_row = bct[H + h:H + h + 1, :]
        ig_row = zt[h:h + 1, :]
        g = b_col[L - 1:L, :]
        m_prev = m_ref[h, 0:1, 0:1]
        n_prev = n_ref[h, 0:1, :]

        a_col = g - b_col + ig_col
        m_loc = jnp.max(a_col, axis=0, keepdims=True)
        w_loc = jnp.exp(a_col - m_loc)

        log_d = jnp.where(causal, b_col - b_row + ig_row, -jnp.inf)
        log_inter = b_col + m_prev
        mt = jnp.maximum(log_inter, jnp.max(log_d, axis=-1, keepdims=True))
        d = jnp.exp(log_d - mt)
        s_inter = jnp.exp(log_inter - mt)

        s = lax.dot_general(qb, kb, (((1,), (1,)), ((), ())), preferred_element_type=f32) * d
        num = (jnp.dot(s.astype(bf16), vb, preferred_element_type=f32)
               + s_inter * jnp.dot(qb, c_ref[h].astype(bf16), preferred_element_type=f32))
        den = (jnp.sum(s, axis=-1, keepdims=True)
               + s_inter * jnp.sum(qb.astype(f32) * n_prev, axis=-1, keepdims=True))
        hh = num / jnp.maximum(jnp.abs(den), jnp.exp(-mt))

        m_new = jnp.maximum(g + m_prev, m_loc)
        s_old = jnp.exp(g + m_prev - m_new)
        s_loc = jnp.exp(m_loc - m_new)
        kw = kb.astype(f32) * w_loc
        kv = jnp.dot(kw.T.astype(bf16), vb, preferred_element_type=f32)
        c_ref[h] = s_old * c_ref[h] + s_loc * kv
        n_ref[h, 0:1, :] = s_old * n_prev + s_loc * jnp.sum(kw, axis=0, keepdims=True)
        m_ref[h] = jnp.broadcast_to(m_new, m_ref.shape[1:])

        mu = jnp.mean(hh, axis=-1, keepdims=True)
        xc = hh - mu
        var = jnp.mean(xc * xc, axis=-1, keepdims=True)
        hn = xc * lax.rsqrt(var + EPS) * gn_ref[:, sl]
        out_ref[:, sl] = (so_ref[:, sl].astype(f32) * hn).astype(out_ref.dtype)


def _mlstm(qk, v, so, pif, bg_pad, gn_m, *, B, S, H):
    T, W = v.shape
    L = M_CHUNK
    Dh = W // H
    BB = 1
    seq = lambda a: a.reshape(B, S, a.shape[-1])
    blk = lambda width, cb: pl.BlockSpec((BB, L, width), lambda b, c: (b, c, cb))
    out = pl.pallas_call(
        functools.partial(_mlstm_kernel, H=H, Dh=Dh),
        grid=(B // BB, S // L),
        in_specs=[blk(W, 0), blk(W, 1), blk(W, 0), blk(W, 0), blk(LANES, 0),
                  pl.BlockSpec((1, LANES), lambda b, c: (0, 0)),
                  pl.BlockSpec((1, W), lambda b, c: (0, 0))],
        out_specs=blk(W, 0),
        out_shape=jax.ShapeDtypeStruct((B, S, W), bf16),
        scratch_shapes=[pltpu.VMEM((BB, H, Dh, Dh), f32),
                        pltpu.VMEM((BB, H, SUBLANES, Dh), f32),
                        pltpu.VMEM((BB, H, SUBLANES, LANES), f32)],
        compiler_params=_params(("parallel", "arbitrary"), 32),
        name="mlstm",
    )(seq(qk), seq(qk), seq(v), seq(so), seq(pif), bg_pad, gn_m)
    return out.reshape(T, W)


def _merge_kernel(hm_ref, ug_ref, vn_ref, gm_ref, gg_ref, x_ref, ws_ref, bst_ref, wbm_ref, wbg_ref, wout_ref,
                  gffn_ref, wr_ref, br_ref, x1_ref, hn2_ref, lg_ref, hg_ref):
    tm = x_ref.shape[0]
    G, L, _ = ws_ref.shape
    Gd = ug_ref.shape[1] // G
    for c in range(tm // L):
        rows = slice(c * L, (c + 1) * L)
        for g in range(G):
            cols = slice(g * Gd, (g + 1) * Gd)
            mixed = jnp.dot(ws_ref[g], vn_ref[rows, cols], preferred_element_type=f32) + bst_ref[:, g:g + 1]
            hg_ref[rows, cols] = (ug_ref[rows, cols].astype(f32) * mixed).astype(hg_ref.dtype)

    a = jnp.dot(hm_ref[...], wbm_ref[...], preferred_element_type=f32)
    b = jnp.dot(hg_ref[...], wbg_ref[...], preferred_element_type=f32)
    merged = gm_ref[...].astype(f32) * a + gg_ref[...].astype(f32) * b
    x1 = x_ref[...] + jnp.dot(merged.astype(bf16), wout_ref[...], preferred_element_type=f32)
    x1_ref[...] = x1
    hn2 = _rms(x1, gffn_ref[...])
    hn2_ref[...] = hn2
    lg_ref[...] = jnp.dot(hn2.astype(bf16), wr_ref[...], preferred_element_type=f32) + br_ref[...]


def _merge(h_m, ug, vn, gates, x2d, ws_tril, b_st, w_bm, w_bg, w_out, g_ffn, w_r, b_r):
    T, D = x2d.shape
    W = h_m.shape[1]
    GW = ug.shape[1]
    G, L, _ = ws_tril.shape
    tm = min(256, T)
    row = lambda i: (i, 0)
    return pl.pallas_call(
        _merge_kernel,
        grid=(T // tm,),
        in_specs=[pl.BlockSpec((tm, W), row),
                  pl.BlockSpec((tm, GW), row),
                  pl.BlockSpec((tm, GW), row),
                  pl.BlockSpec((tm, D), lambda i: (i, 0)),
                  pl.BlockSpec((tm, D), lambda i: (i, 1)),
                  pl.BlockSpec((tm, D), row),
                  _const_spec((G, L, L)), _const_spec((L, G)),
                  _const_spec((W, D)), _const_spec((GW, D)), _const_spec((D, D)),
                  _const_spec((1, D)), _const_spec((D, LANES)), _const_spec((1, LANES))],
        out_specs=[pl.BlockSpec((tm, D), row), pl.BlockSpec((tm, D), row), pl.BlockSpec((tm, LANES), row)],
        out_shape=[jax.ShapeDtypeStruct((T, D), f32), jax.ShapeDtypeStruct((T, D), f32),
                   jax.ShapeDtypeStruct((T, LANES), f32)],
        scratch_shapes=[pltpu.VMEM((tm, GW), bf16)],
        compiler_params=_params(("parallel",), 48),
        name="merge",
    )(h_m, ug, vn, gates, gates, x2d, ws_tril, b_st, w_bm, w_bg, w_out, g_ffn, w_r, b_r)


def _route_kernel(lg_ref, info_ref, cnt_ref, carry_ref, *, NG, EPG):
    tm = lg_ref.shape[0]

    @pl.when(pl.program_id(0) == 0)
    def _():
        carry_ref[...] = jnp.zeros_like(carry_ref)

    lg = lg_ref[...]
    lane = lax.broadcasted_iota(jnp.int32, (tm, LANES), 1)
    lanef = lane.astype(f32)
    big = float(LANES)

    def first_max(vals):
        mx = jnp.max(vals, axis=-1, keepdims=True)
        idx = jnp.min(jnp.where(vals == mx, lanef, big), axis=-1, keepdims=True)
        return mx, idx

    is_g = lane < NG
    gmax, grp = first_max(jnp.where(is_g, lg, -jnp.inf))
    p_grp = 1.0 / jnp.sum(jnp.where(is_g, jnp.exp(lg - gmax), 0.0), axis=-1, keepdims=True)
    lo = NG + grp * EPG
    el = jnp.where(jnp.logical_and(lanef >= lo, lanef < lo + EPG), lg, -jnp.inf)
    v1, i1 = first_max(el)
    v2, i2 = first_max(jnp.where(lanef == i1, -jnp.inf, el))
    t = jnp.exp(v2 - v1)
    w1 = p_grp / (1.0 + t)
    w2 = p_grp * t / (1.0 + t)

    hit1 = lanef == i1
    hit2 = lanef == i2
    oh = jnp.where(jnp.logical_or(hit1, hit2), 1.0, 0.0)
    r = lax.broadcasted_iota(jnp.int32, (tm, tm), 0)
    c = lax.broadcasted_iota(jnp.int32, (tm, tm), 1)
    before = jnp.where(r > c, 1.0, 0.0).astype(bf16)
    carry = carry_ref[0:1, :]
    earlier = jnp.dot(before, oh.astype(bf16), preferred_element_type=f32) + carry
    rank1 = jnp.sum(jnp.where(hit1, earlier, 0.0), axis=-1, keepdims=True)
    rank2 = jnp.sum(jnp.where(hit2, earlier, 0.0), axis=-1, keepdims=True)
    new_carry = carry + jnp.sum(oh, axis=0, keepdims=True)
    carry_ref[...] = jnp.broadcast_to(new_carry, carry_ref.shape)
    cnt_ref[...] = jnp.broadcast_to(new_carry, cnt_ref.shape)

    vals = (i1 - NG, i2 - NG, w1, w2, rank1, rank2)
    info = jnp.zeros((tm, LANES), f32)
    for k, v in enumerate(vals):
        info = jnp.where(lane == k, v, info)
    info_ref[...] = info


def _route(logits, *, NG, EPG):
    T = logits.shape[0]
    tm = min(512, T)
    return pl.pallas_call(
        functools.partial(_route_kernel, NG=NG, EPG=EPG),
        grid=(T // tm,),
        in_specs=[pl.BlockSpec((tm, LANES), lambda i: (i, 0))],
        out_specs=[pl.BlockSpec((tm, LANES), lambda i: (i, 0)),
                   pl.BlockSpec((SUBLANES, LANES), lambda i: (0, 0))],
        out_shape=[jax.ShapeDtypeStruct((T, LANES), f32), jax.ShapeDtypeStruct((SUBLANES, LANES), f32)],
        scratch_shapes=[pltpu.VMEM((SUBLANES, LANES), f32)],
        compiler_params=_params(("arbitrary",), 32),
        name="route",
    )(logits)


def _dest_kernel(info_ref, ps_ref, dest_ref):
    tm = info_ref.shape[0]
    info = info_ref[...]
    lane = lax.broadcasted_iota(jnp.int32, (tm, LANES), 1)
    lanef = lane.astype(f32)
    ps = ps_ref[...]
    d1 = jnp.sum(jnp.where(lanef == info[:, 0:1], ps, 0.0), axis=-1, keepdims=True) + info[:, 4:5]
    d2 = jnp.sum(jnp.where(lanef == info[:, 1:2], ps, 0.0), axis=-1, keepdims=True) + info[:, 5:6]
    dest_ref[...] = jnp.where(lane == 0, d1, jnp.where(lane == 1, d2, 0.0)).astype(jnp.int32)


def _dest(info, pstart_row):
    T = info.shape[0]
    tm = min(512, T)
    return pl.pallas_call(
        _dest_kernel,
        grid=(T // tm,),
        in_specs=[pl.BlockSpec((tm, LANES), lambda i: (i, 0)), pl.BlockSpec((1, LANES), lambda i: (0, 0))],
        out_specs=pl.BlockSpec((tm, LANES), lambda i: (i, 0)),
        out_shape=jax.ShapeDtypeStruct((T, LANES), jnp.int32),
        compiler_params=_params(("parallel",), 32),
        name="dest",
    )(info, pstart_row)


def _row_copy(src, si, dst, di, sem, n=1):
    return pltpu.make_async_copy(src.at[pl.ds(si, n)], dst.at[pl.ds(di, n)], sem)


def _fill_rows(seg_ref, zero_ref, xs_ref, sem, *, wait):
    zrows = zero_ref.shape[0]

    def act(start, n):
        cp = _row_copy(zero_ref, 0, xs_ref, start, sem, n)
        cp.wait() if wait else cp.start()

    def aligned(start, n):
        act(pl.multiple_of(start, SUBLANES), n)

    def per_segment(e, carry):
        lo = seg_ref[0, e]
        hi = seg_ref[1, e]
        lo_al = jnp.minimum((lo + SUBLANES - 1) // SUBLANES * SUBLANES, hi)

        def single(r, cc):
            act(r, 1)
            return cc

        lax.fori_loop(lo, lo_al, single, 0)
        count = hi - lo_al
        nfull = count // zrows

        def full(c, cc):
            aligned(lo_al + c * zrows, zrows)
            return cc

        lax.fori_loop(0, nfull, full, 0)
        pos = lo_al + nfull * zrows
        rem = count - nfull * zrows
        n = zrows // 2
        while n >= SUBLANES:
            take = (rem & n) != 0
            pl.when(take)(functools.partial(aligned, pos, n))
            pos = pos + jnp.where(take, n, 0)
            n //= 2
        return carry

    lax.fori_loop(0, seg_ref.shape[1], per_segment, 0)


def _dispatch_kernel(seg_ref, dest_ref, hn_ref, xs_ref, zero_ref, sem, zsem):
    tm = hn_ref.shape[0]

    @pl.when(pl.program_id(0) == 0)
    def _():
        zero_ref[...] = jnp.zeros_like(zero_ref)
        _fill_rows(seg_ref, zero_ref, xs_ref, zsem, wait=False)
        _fill_rows(seg_ref, zero_ref, xs_ref, zsem, wait=True)

    for r in range(tm):
        for k in range(TOP_K):
            _row_copy(hn_ref, r, xs_ref, dest_ref[0, k, r], sem).start(priority=k % 2)

    def drain(r, carry):
        for k in range(TOP_K):
            _row_copy(hn_ref, 0, xs_ref, 0, sem).wait()
        return carry

    lax.fori_loop(0, tm, drain, 0, unroll=8)


def _dispatch(seg, dest_blocks, hn2, n_rows):
    T, D = hn2.shape
    tm = dest_blocks.shape[2]
    grid_spec = pltpu.PrefetchScalarGridSpec(
        num_scalar_prefetch=1,
        grid=(T // tm,),
        in_specs=[pl.BlockSpec((1, TOP_K, tm), lambda i, seg: (i, 0, 0), memory_space=pltpu.SMEM),
                  pl.BlockSpec((tm, D), lambda i, seg: (i, 0))],
        out_specs=pl.BlockSpec(memory_space=pl.ANY),
        scratch_shapes=[pltpu.VMEM((EXPERT_BLOCK // 2, D), hn2.dtype), pltpu.SemaphoreType.DMA(()),
                        pltpu.SemaphoreType.DMA(())],
    )
    return pl.pallas_call(
        _dispatch_kernel,
        grid_spec=grid_spec,
        out_shape=jax.ShapeDtypeStruct((n_rows, D), hn2.dtype),
        compiler_params=_params(("arbitrary",), 32),
        name="dispatch",
    )(seg, dest_blocks, hn2)


def _experts_kernel(be_ref, nu_ref, re_ref, xs_ref, w1_ref, w3_ref, w2_ref, ys_ref, st1, st3, st2, wb1, wb3, wb2, sems):
    i = pl.program_id(0)
    e = be_ref[i]
    prev = be_ref[jnp.maximum(i - 1, 0)]
    run_end = re_ref[e]

    def fetch(ex):
        return [pltpu.make_async_copy(w1_ref.at[ex], st1, sems.at[0]),
                pltpu.make_async_copy(w3_ref.at[ex], st3, sems.at[1]),
                pltpu.make_async_copy(w2_ref.at[ex], st2, sems.at[2])]

    @pl.when(i == 0)
    def _():
        for cp in fetch(e):
            cp.start()

    @pl.when(jnp.logical_or(i == 0, jnp.logical_and(e != prev, i < nu_ref[0])))
    def _():
        for cp in fetch(e):
            cp.wait()
        wb1[...] = st1[...].astype(bf16)
        wb3[...] = st3[...].astype(bf16)
        wb2[...] = st2[...].astype(bf16)
        @pl.when(run_end < nu_ref[0])
        def _():
            for cp in fetch(be_ref[jnp.minimum(run_end, pl.num_programs(0) - 1)]):
                cp.start()

    @pl.when(i < nu_ref[0])
    def _():
        x = xs_ref[...].astype(bf16)
        h1 = jnp.dot(x, wb1[...], preferred_element_type=f32)
        h3 = jnp.dot(x, wb3[...], preferred_element_type=f32)
        a = (h1 * jax.nn.sigmoid(h1) * h3).astype(bf16)
        ys_ref[...] = jnp.dot(a, wb2[...], preferred_element_type=f32)

    @pl.when(i >= nu_ref[0])
    def _():
        ys_ref[...] = jnp.zeros_like(ys_ref)


def _experts(blk_e, n_used, run_ends, xs, w1, w3, w2):
    P, D = xs.shape
    NE, _, De = w1.shape
    BM = EXPERT_BLOCK
    grid_spec = pltpu.PrefetchScalarGridSpec(
        num_scalar_prefetch=3,
        grid=(P // BM,),
        in_specs=[pl.BlockSpec((BM, D), lambda i, be, nu, nx: (jnp.minimum(i, nu[0] - 1), 0)),
                  pl.BlockSpec(memory_space=pl.ANY), pl.BlockSpec(memory_space=pl.ANY),
                  pl.BlockSpec(memory_space=pl.ANY)],
        out_specs=pl.BlockSpec((BM, D), lambda i, be, nu, nx: (i, 0)),
        scratch_shapes=[pltpu.VMEM((D, De), w1.dtype), pltpu.VMEM((D, De), w3.dtype), pltpu.VMEM((De, D), w2.dtype),
                        pltpu.VMEM((D, De), bf16), pltpu.VMEM((D, De), bf16), pltpu.VMEM((De, D), bf16),
                        pltpu.SemaphoreType.DMA((3,))],
    )
    return pl.pallas_call(
        _experts_kernel,
        grid_spec=grid_spec,
        out_shape=jax.ShapeDtypeStruct((P, D), f32),
        compiler_params=_params(("arbitrary",), 48),
        name="experts",
    )(blk_e, n_used, run_ends, xs, w1, w3, w2)


def _final_kernel(dcur_ref, dnxt_ref, x1_ref, p_ref, info_ref, ys_ref, wup_ref, wg_ref, gple_ref, gfin_ref, out_ref,
                  buf_ref, x2_ref, sems, *, final_norm):
    tm = x1_ref.shape[0]
    i = pl.program_id(0)
    slot = lax.rem(i, 2)

    def gather(dref, r, s):
        return [_row_copy(ys_ref, dref[0, k, r], buf_ref.at[s, k], r, sems.at[s]) for k in range(TOP_K)]

    def drain(s):
        def body(r, carry):
            for cp in gather(dcur_ref, 0, s):
                cp.wait()
            return carry
        lax.fori_loop(0, tm, body, 0, unroll=8)

    @pl.when(i == 0)
    def _():
        def body(r, carry):
            for cp in gather(dcur_ref, r, 0):
                cp.start()
            return carry
        lax.fori_loop(0, tm, body, 0, unroll=8)

    drain(slot)
    info = info_ref[...]
    x2_ref[...] = x1_ref[...] + buf_ref[slot, 0] * info[:, 2:3] + buf_ref[slot, 1] * info[:, 3:4]

    for r in range(tm):
        for k, cp in enumerate(gather(dnxt_ref, r, 1 - slot)):
            cp.start(priority=k % 2)

    ple = jnp.dot(p_ref[...].astype(bf16), wup_ref[...], preferred_element_type=f32)
    x2 = x2_ref[...]
    hn3 = _rms(x2, gple_ref[...]).astype(bf16)
    gate = jax.nn.sigmoid(jnp.dot(hn3, wg_ref[...], preferred_element_type=f32))
    x3 = x2 + gate * ple
    out_ref[...] = _rms(x3, gfin_ref[...]) if final_norm else x3

    @pl.when(i == pl.num_programs(0) - 1)
    def _():
        drain(1 - slot)


def _final(dest_blocks, x1, p2d, info, ys, w_up, w_gate, g_ple, g_final, *, final_norm):
    T, D = x1.shape
    PD = p2d.shape[1]
    tm = dest_blocks.shape[2]
    nt = T // tm
    row = lambda i: (i, 0)
    return pl.pallas_call(
        functools.partial(_final_kernel, final_norm=final_norm),
        grid=(nt,),
        in_specs=[pl.BlockSpec((1, TOP_K, tm), lambda i: (i, 0, 0), memory_space=pltpu.SMEM),
                  pl.BlockSpec((1, TOP_K, tm), lambda i: (jnp.minimum(i + 1, nt - 1), 0, 0),
                               memory_space=pltpu.SMEM),
                  pl.BlockSpec((tm, D), row),
                  pl.BlockSpec((tm, PD), row),
                  pl.BlockSpec((tm, LANES), row),
                  pl.BlockSpec(memory_space=pl.ANY),
                  _const_spec((PD, D)), _const_spec((D, D)), _const_spec((1, D)), _const_spec((1, D))],
        out_specs=pl.BlockSpec((tm, D), row),
        out_shape=jax.ShapeDtypeStruct((T, D), f32),
        scratch_shapes=[pltpu.VMEM((2, TOP_K, tm, D), ys.dtype), pltpu.VMEM((tm, D), f32),
                        pltpu.SemaphoreType.DMA((2,))],
        compiler_params=_params(("arbitrary",), 44),
        name="final",
    )(dest_blocks, dest_blocks, x1, p2d, info, ys, w_up, w_gate, g_ple, g_final)


def _layer(x2d, p2d, g_mix, w_in, conv_w, conv_b, b_gate, gn_m, ln_g, ln_b, w_s, b_s, w_bm, w_bg, w_out,
           g_ffn, w_rg, b_rg, w_re, b_re, w1, w3, w2, g_ple, w_ple_up, w_ple_gate, g_out, *, B, S, final_norm):
    T, D = x2d.shape
    H = b_gate.shape[0] // 2
    W = gn_m.shape[0]
    GW = ln_g.shape[0]
    NG = w_rg.shape[1]
    NE = w_re.shape[1]
    EPG = NE // NG
    BM = EXPERT_BLOCK
    KW = conv_w.shape[0]
    tn = PROJ_COLS
    row2 = lambda v: v.reshape(1, -1).astype(f32)

    n_if = 2 * H
    c_if = 4 * W
    c_uv = c_if + n_if
    w_bf = w_in.astype(bf16)
    w_tail = w_bf[:, c_uv:]
    w_if = jnp.pad(w_bf[:, c_if:c_uv], ((0, 0), (0, LANES - n_if)))
    bg_pad = jnp.concatenate([b_gate.astype(f32), jnp.zeros((LANES - n_if,), f32)]).reshape(1, LANES)

    tm = min(PROJ_ROWS, S)
    hspec = lambda shape: pl.BlockSpec(shape, lambda i, j: (0, j))
    qk, hn = pl.pallas_call(
        functools.partial(_proj_qk_kernel, tiles_per_seq=S // tm, KW=KW, k_scale=(W // H) ** -0.5),
        grid=(T // tm, 2 * W // tn),
        in_specs=[pl.BlockSpec((tm, D), lambda i, j: (i, 0)), pl.BlockSpec((1, D), lambda i, j: (0, 0)),
                  hspec((D, tn)), hspec((KW, tn)), hspec((1, tn))],
        out_specs=[pl.BlockSpec((tm, tn), lambda i, j: (i, j)), pl.BlockSpec((tm, D), lambda i, j: (i, 0))],
        out_shape=[jax.ShapeDtypeStruct((T, 2 * W), bf16), jax.ShapeDtypeStruct((T, D), bf16)],
        scratch_shapes=[pltpu.VMEM((2 * W // tn, SUBLANES, tn), f32)],
        compiler_params=_params(("arbitrary", "arbitrary"), 52),
        name="proj_qk",
    )(x2d, row2(g_mix), w_bf, conv_w.astype(f32), row2(conv_b))
    v, pif = _proj("proj_v", _proj_v_kernel, hn, w_bf, 2 * W // tn, W // tn, tm=tm,
                   extra=(w_if,), extra_specs=(pl.BlockSpec((D, LANES), lambda i, j: (0, 0)),),
                   extra_out=(jax.ShapeDtypeStruct((T, LANES), f32),),
                   extra_out_specs=(pl.BlockSpec((tm, LANES), lambda i, j: (i, 0)),))
    so = _proj("proj_o", functools.partial(_proj_act_kernel, act=jax.nn.sigmoid), hn, w_bf, 3 * W // tn, W // tn, tm=tm)
    ug = _proj("proj_u", functools.partial(_proj_act_kernel, act=jax.nn.gelu), hn, w_tail, 0, GW // tn, tm=tm)
    vn = _proj("proj_vn", _proj_vn_kernel, hn, w_tail, GW // tn, 1, tm=tm,
               extra=(row2(ln_g), row2(ln_b)), extra_specs=(hspec((1, tn)), hspec((1, tn))),
               scratch=(pltpu.VMEM((tm, tn), f32),))
    gates = _proj("proj_gates", functools.partial(_proj_act_kernel, act=jax.nn.sigmoid), hn, w_tail,
                  2 * GW // tn, 2 * D // tn, tm=tm)

    h_m = _mlstm(qk, v, so, pif, bg_pad, row2(gn_m), B=B, S=S, H=H)

    w_r = jnp.concatenate([w_rg, w_re, jnp.zeros((D, LANES - NG - NE), w_rg.dtype)], axis=1).astype(bf16)
    b_r = jnp.concatenate([b_rg.astype(f32), b_re.astype(f32), jnp.zeros((LANES - NG - NE,), f32)]).reshape(1, LANES)
    x1, hn2, logits = _merge(h_m, ug, vn, gates, x2d, jnp.tril(w_s).astype(bf16), jnp.transpose(b_s).astype(f32),
                             w_bm.astype(bf16), w_bg.astype(bf16), w_out.astype(bf16), row2(g_ffn), w_r, b_r)

    info, cnt = _route(logits, NG=NG, EPG=EPG)
    counts = cnt[0, NG:NG + NE].astype(jnp.int32)
    padded = (counts + BM - 1) // BM * BM
    pends = jnp.cumsum(padded)
    pstarts = pends - padded
    NB = (T * TOP_K) // BM + NE
    blk_start = jnp.arange(NB, dtype=jnp.int32) * BM
    blk_e = jnp.minimum(jnp.sum((pends[None, :] <= blk_start[:, None]).astype(jnp.int32), axis=1), NE - 1)
    n_used = (pends[-1:] // BM).astype(jnp.int32)
    run_ends = (pends // BM).astype(jnp.int32)
    ps_row = jnp.zeros((1, LANES), f32).at[0, :NE].set(pstarts.astype(f32))
    seg = jnp.stack([jnp.concatenate([pstarts + counts, pends[-1:]]),
                     jnp.concatenate([pends, jnp.full((1,), NB * BM, jnp.int32)])]).astype(jnp.int32)

    dest = _dest(info, ps_row)
    tmd = min(256, T)
    dest_blocks = dest[:, :TOP_K].reshape(T // tmd, tmd, TOP_K).transpose(0, 2, 1)

    xs = _dispatch(seg, dest_blocks, hn2, NB * BM)
    ys = _experts(blk_e, n_used, run_ends, xs, w1, w3, w2)
    return _final(dest_blocks, x1, p2d, info, ys, w_ple_up.astype(bf16), w_ple_gate.astype(bf16),
                  row2(g_ple), row2(g_out), final_norm=final_norm)


def kernel(x, p, g_mix, w_in, conv_w, conv_b, b_gate, gn_m, ln_g, ln_b, w_s, b_s, w_bm, w_bg, w_out, g_ffn, w_rg, b_rg, w_re, b_re, w1, w3, w2, g_ple, w_ple_up, w_ple_gate, g_final):
    B, S, D = x.shape
    depth = w_in.shape[0]
    x2d = x.reshape(B * S, D)
    for i in range(depth):
        last = i == depth - 1
        x2d = _layer(x2d, p[i].reshape(B * S, -1), g_mix[i], w_in[i], conv_w[i], conv_b[i], b_gate[i], gn_m[i],
                     ln_g[i], ln_b[i], w_s[i], b_s[i], w_bm[i], w_bg[i], w_out[i], g_ffn[i], w_rg[i], b_rg[i],
                     w_re[i], b_re[i], w1[i], w3[i], w2[i], g_ple[i], w_ple_up[i], w_ple_gate[i], g_final,
                     B=B, S=S, final_norm=last)
    return x2d.reshape(B, S, D)
```

```python
import functools

import jax
import jax.numpy as jnp
from jax import lax
from jax.experimental import pallas as pl
from jax.experimental.pallas import tpu as pltpu

EPS = 1e-6
M_CHUNK = 128
TOP_K = 2
LANES = 128
SUBLANES = 8
MXU_DIM = 256
EXPERT_BLOCK = 256
PROJ_ROWS = 1024
PROJ_COLS = 1024
MIB = 1024 * 1024

f32 = jnp.float32
bf16 = jnp.bfloat16


def _params(semantics, vmem_mib):
    return pltpu.CompilerParams(dimension_semantics=semantics, vmem_limit_bytes=vmem_mib * MIB)


def _const_spec(shape):
    nd = len(shape)
    return pl.BlockSpec(shape, lambda *_: (0,) * nd, pipeline_mode=pl.Buffered(1))


def _rms(x, g):
    ms = jnp.mean(x * x, axis=-1, keepdims=True)
    return x * lax.rsqrt(ms + EPS) * g


def _chunk_cols(n):
    return slice(n * MXU_DIM, (n + 1) * MXU_DIM)


def _dot_chunk(hn_ref, w_ref, n):
    return jnp.dot(hn_ref[...], w_ref[:, _chunk_cols(n)], preferred_element_type=f32)


def _proj_act_kernel(hn_ref, w_ref, o_ref, *, act):
    for n in range(o_ref.shape[1] // MXU_DIM):
        o_ref[:, _chunk_cols(n)] = act(_dot_chunk(hn_ref, w_ref, n)).astype(o_ref.dtype)


def _proj_v_kernel(hn_ref, w_ref, wif_ref, o_ref, oif_ref):
    _proj_act_kernel(hn_ref, w_ref, o_ref, act=lambda t: t)
    oif_ref[...] = jnp.dot(hn_ref[...], wif_ref[...], preferred_element_type=f32)


def _proj_qk_kernel(x_ref, g_ref, w_ref, cw_ref, cb_ref, o_ref, hn_ref, carry_ref, *, tiles_per_seq, KW, k_scale):
    i = pl.program_id(0)
    j = pl.program_id(1)
    tm = hn_ref.shape[0]
    scale = jnp.where(j == 1, k_scale, 1.0).astype(f32)

    @pl.when(j == 0)
    def _():
        hn_ref[...] = _rms(x_ref[...], g_ref[...]).astype(hn_ref.dtype)

    @pl.when(lax.rem(i, tiles_per_seq) == 0)
    def _():
        carry_ref[j] = jnp.zeros(carry_ref.shape[1:], f32)

    for n in range(o_ref.shape[1] // MXU_DIM):
        cols = _chunk_cols(n)
        acc = _dot_chunk(hn_ref, w_ref, n)
        prev = carry_ref[j, :, cols]
        carry_ref[j, :, cols] = acc[tm - SUBLANES:, :]
        ext = jnp.concatenate([prev, acc], axis=0)
        y = cb_ref[:, cols] + cw_ref[KW - 1:KW, cols] * acc
        for s in range(1, KW):
            y = y + cw_ref[KW - 1 - s:KW - s, cols] * pltpu.roll(ext, s, axis=0)[SUBLANES:, :]
        o_ref[:, cols] = (y * jax.nn.sigmoid(y) * scale).astype(o_ref.dtype)


def _proj_vn_kernel(hn_ref, w_ref, lng_ref, lnb_ref, o_ref, g_ref):
    nch = o_ref.shape[1] // MXU_DIM
    width = o_ref.shape[1]
    total = 0.0
    for n in range(nch):
        g = jax.nn.gelu(_dot_chunk(hn_ref, w_ref, n))
        g_ref[:, _chunk_cols(n)] = g
        total = total + jnp.sum(g, axis=-1, keepdims=True)
    mu = total / width
    sq = 0.0
    for n in range(nch):
        xc = g_ref[:, _chunk_cols(n)] - mu
        sq = sq + jnp.sum(xc * xc, axis=-1, keepdims=True)
    inv = lax.rsqrt(sq / width + EPS)
    for n in range(nch):
        cols = _chunk_cols(n)
        o_ref[:, cols] = ((g_ref[:, cols] - mu) * inv * lng_ref[:, cols] + lnb_ref[:, cols]).astype(o_ref.dtype)


def _proj(name, body, hn, w, col0, ncol, *, extra=(), extra_specs=(), extra_out=(), extra_out_specs=(), scratch=(),
          tm):
    T, D = hn.shape
    tn = PROJ_COLS
    outs = pl.pallas_call(
        body,
        grid=(T // tm, ncol),
        in_specs=[pl.BlockSpec((tm, D), lambda i, j: (i, 0)),
                  pl.BlockSpec((D, tn), lambda i, j: (0, col0 + j))] + list(extra_specs),
        out_specs=[pl.BlockSpec((tm, tn), lambda i, j: (i, j))] + list(extra_out_specs),
        out_shape=[jax.ShapeDtypeStruct((T, ncol * tn), bf16)] + list(extra_out),
        scratch_shapes=list(scratch),
        compiler_params=_params(("arbitrary", "arbitrary"), 48),
        name=name,
    )(hn, w, *extra)
    return outs if extra_out else outs[0]


def _mlstm_kernel(q_ref, k_ref, v_ref, so_ref, if_ref, bg_ref, gn_ref, out_ref, c_ref, m_ref, *, H, Dh):
    @pl.when(pl.program_id(1) == 0)
    def _():
        c_ref[...] = jnp.zeros_like(c_ref)
        m_ref[...] = jnp.zeros_like(m_ref)

    for bb in range(q_ref.shape[0]):
        _mlstm_chunk(q_ref.at[bb], k_ref.at[bb], v_ref.at[bb], so_ref.at[bb], if_ref.at[bb], bg_ref, gn_ref,
                     out_ref.at[bb], c_ref.at[bb], m_ref.at[bb], H=H, Dh=Dh)


def _mlstm_chunk(q_ref, k_ref, v_ref, so_ref, if_ref, bg_ref, gn_ref, out_ref, c_ref, m_ref, *, H, Dh):
    L = M_CHUNK
    gp = if_ref[...] + bg_ref[...]
    lane = lax.broadcasted_iota(jnp.int32, (L, LANES), 1)
    lf = jnp.minimum(gp, 0.0) - jnp.log1p(jnp.exp(-jnp.abs(gp)))
    z = jnp.where(lane < H, gp, lf)
    row = lax.broadcasted_iota(jnp.int32, (L, L), 0)
    col = lax.broadcasted_iota(jnp.int32, (L, L), 1)
    causal = row >= col
    bc = jnp.dot(causal.astype(f32), z, precision=lax.Precision.HIGHEST, preferred_element_type=f32)
    zt = z.T
    bct = bc.T

    ones = jnp.ones((L, LANES), bf16)
    heads = range(H)
    sl = [slice(h * Dh, (h + 1) * Dh) for h in heads]
    qb = [q_ref[:, sl[h]] for h in heads]
    kb = [k_ref[:, sl[h]] for h in heads]
    va = [jnp.concatenate([v_ref[:, sl[h]], ones], axis=1) for h in heads]
    b_col = [bc[:, H + h:H + h + 1] for h in heads]
    g = [b_col[h][L - 1:L, :] for h in heads]
    m_prev = [m_ref[h, 0:1, 0:1] for h in heads]

    a_col = [g[h] - b_col[h] + z[:, h:h + 1] for h in heads]
    m_loc = [jnp.max(a_col[h], axis=0, keepdims=True) for h in heads]
    w_loc = [jnp.exp(a_col[h] - m_loc[h]) for h in heads]
    log_d = [jnp.where(causal, b_col[h] - bct[H + h:H + h + 1, :] + zt[h:h + 1, :], -jnp.inf) for h in heads]
    log_inter = [b_col[h] + m_prev[h] for h in heads]
    mt = [jnp.maximum(log_inter[h], jnp.max(log_d[h], axis=-1, keepdims=True)) for h in heads]
    d = [jnp.exp(log_d[h] - mt[h]) for h in heads]
    s_inter = [jnp.exp(log_inter[h] - mt[h]) for h in heads]

    s = [lax.dot_general(qb[h], kb[h], (((1,), (1,)), ((), ())), preferred_element_type=f32) * d[h] for h in heads]
    inter = [jnp.dot(qb[h], c_ref[h].astype(bf16), preferred_element_type=f32) for h in heads]
    nd = [jnp.dot(s[h].astype(bf16), va[h], preferred_element_type=f32) + s_inter[h] * inter[h]
          for h in heads]
    hh = [nd[h][:, :Dh] / jnp.maximum(jnp.abs(nd[h][:, Dh:Dh + 1]), jnp.exp(-mt[h])) for h in heads]

    m_new = [jnp.maximum(g[h] + m_prev[h], m_loc[h]) for h in heads]
    kv = [jnp.dot((kb[h].astype(f32) * w_loc[h]).T.astype(bf16), va[h], preferred_element_type=f32) for h in heads]
    for h in heads:
        c_ref[h] = jnp.exp(g[h] + m_prev[h] - m_new[h]) * c_ref[h] + jnp.exp(m_loc[h] - m_new[h]) * kv[h]
        m_ref[h] = jnp.broadcast_to(m_new[h], m_ref.shape[1:])

    mu = [jnp.mean(hh[h], axis=-1, keepdims=True) for h in heads]
    xc = [hh[h] - mu[h] for h in heads]
    var = [jnp.mean(xc[h] * xc[h], axis=-1, keepdims=True) for h in heads]
    for h in heads:
        hn = xc[h] * lax.rsqrt(var[h] + EPS) * gn_ref[:, sl[h]]
        out_ref[:, sl[h]] = (so_ref[:, sl[h]].astype(f32) * hn).astype(out_ref.dtype)


def _mlstm(qk, v, so, pif, bg_pad, gn_m, *, B, S, H):
    T, W = v.shape
    L = M_CHUNK
    Dh = W // H
    BB = 1
    seq = lambda a: a.reshape(B, S, a.shape[-1])
    blk = lambda width, cb: pl.BlockSpec((BB, L, width), lambda b, c: (b, c, cb))
    out = pl.pallas_call(
        functools.partial(_mlstm_kernel, H=H, Dh=Dh),
        grid=(B // BB, S // L),
        in_specs=[blk(W, 0), blk(W, 1), blk(W, 0), blk(W, 0), blk(LANES, 0),
                  pl.BlockSpec((1, LANES), lambda b, c: (0, 0)),
                  pl.BlockSpec((1, W), lambda b, c: (0, 0))],
        out_specs=blk(W, 0),
        out_shape=jax.ShapeDtypeStruct((B, S, W), bf16),
        scratch_shapes=[pltpu.VMEM((BB, H, Dh, Dh + LANES), f32),
                        pltpu.VMEM((BB, H, SUBLANES, LANES), f32)],
        compiler_params=_params(("parallel", "arbitrary"), 32),
        name="mlstm",
    )(seq(qk), seq(qk), seq(v), seq(so), seq(pif), bg_pad, gn_m)
    return out.reshape(T, W)


def _merge_kernel(hm_ref, ug_ref, vn_ref, gm_ref, gg_ref, x_ref, ws_ref, bst_ref, wbm_ref, wbg_ref, wout_ref,
                  gffn_ref, wr_ref, br_ref, x1_ref, hn2_ref, lg_ref, hg_ref):
    tm = x_ref.shape[0]
    G, L, _ = ws_ref.shape
    Gd = ug_ref.shape[1] // G
    for c in range(tm // L):
        rows = slice(c * L, (c + 1) * L)
        for g in range(G):
            cols = slice(g * Gd, (g + 1) * Gd)
            mixed = jnp.dot(ws_ref[g], vn_ref[rows, cols], preferred_element_type=f32) + bst_ref[:, g:g + 1]
            hg_ref[rows, cols] = (ug_ref[rows, cols].astype(f32) * mixed).astype(hg_ref.dtype)

    a = jnp.dot(hm_ref[...], wbm_ref[...], preferred_element_type=f32)
    b = jnp.dot(hg_ref[...], wbg_ref[...], preferred_element_type=f32)
    merged = gm_ref[...].astype(f32) * a + gg_ref[...].astype(f32) * b
    x1 = x_ref[...] + jnp.dot(merged.astype(bf16), wout_ref[...], preferred_element_type=f32)
    x1_ref[...] = x1
    hn2 = _rms(x1, gffn_ref[...])
    hn2_ref[...] = hn2
    lg_ref[...] = jnp.dot(hn2.astype(bf16), wr_ref[...], preferred_element_type=f32) + br_ref[...]


def _merge(h_m, ug, vn, gates, x2d, ws_tril, b_st, w_bm, w_bg, w_out, g_ffn, w_r, b_r):
    T, D = x2d.shape
    W = h_m.shape[1]
    GW = ug.shape[1]
    G, L, _ = ws_tril.shape
    tm = min(256, T)
    row = lambda i: (i, 0)
    return pl.pallas_call(
        _merge_kernel,
        grid=(T // tm,),
        in_specs=[pl.BlockSpec((tm, W), row),
                  pl.BlockSpec((tm, GW), row),
                  pl.BlockSpec((tm, GW), row),
                  pl.BlockSpec((tm, D), lambda i: (i, 0)),
                  pl.BlockSpec((tm, D), lambda i: (i, 1)),
                  pl.BlockSpec((tm, D), row),
                  _const_spec((G, L, L)), _const_spec((L, G)),
                  _const_spec((W, D)), _const_spec((GW, D)), _const_spec((D, D)),
                  _const_spec((1, D)), _const_spec((D, LANES)), _const_spec((1, LANES))],
        out_specs=[pl.BlockSpec((tm, D), row), pl.BlockSpec((tm, D), row), pl.BlockSpec((tm, LANES), row)],
        out_shape=[jax.ShapeDtypeStruct((T, D), f32), jax.ShapeDtypeStruct((T, D), f32),
                   jax.ShapeDtypeStruct((T, LANES), f32)],
        scratch_shapes=[pltpu.VMEM((tm, GW), bf16)],
        compiler_params=_params(("parallel",), 48),
        name="merge",
    )(h_m, ug, vn, gates, gates, x2d, ws_tril, b_st, w_bm, w_bg, w_out, g_ffn, w_r, b_r)


def _route_kernel(lg_ref, info_ref, cnt_ref, carry_ref, *, NG, EPG):
    tm = lg_ref.shape[0]

    @pl.when(pl.program_id(0) == 0)
    def _():
        carry_ref[...] = jnp.zeros_like(carry_ref)

    lg = lg_ref[...]
    lane = lax.broadcasted_iota(jnp.int32, (tm, LANES), 1)
    lanef = lane.astype(f32)
    big = float(LANES)

    def first_max(vals):
        mx = jnp.max(vals, axis=-1, keepdims=True)
        idx = jnp.min(jnp.where(vals == mx, lanef, big), axis=-1, keepdims=True)
        return mx, idx

    is_g = lane < NG
    gmax, grp = first_max(jnp.where(is_g, lg, -jnp.inf))
    p_grp = 1.0 / jnp.sum(jnp.where(is_g, jnp.exp(lg - gmax), 0.0), axis=-1, keepdims=True)
    lo = NG + grp * EPG
    el = jnp.where(jnp.logical_and(lanef >= lo, lanef < lo + EPG), lg, -jnp.inf)
    v1, i1 = first_max(el)
    v2, i2 = first_max(jnp.where(lanef == i1, -jnp.inf, el))
    t = jnp.exp(v2 - v1)
    w1 = p_grp / (1.0 + t)
    w2 = p_grp * t / (1.0 + t)

    hit1 = lanef == i1
    hit2 = lanef == i2
    oh = jnp.where(jnp.logical_or(hit1, hit2), 1.0, 0.0)
    r = lax.broadcasted_iota(jnp.int32, (tm, tm), 0)
    c = lax.broadcasted_iota(jnp.int32, (tm, tm), 1)
    before = jnp.where(r > c, 1.0, 0.0).astype(bf16)
    carry = carry_ref[0:1, :]
    earlier = jnp.dot(before, oh.astype(bf16), preferred_element_type=f32) + carry
    rank1 = jnp.sum(jnp.where(hit1, earlier, 0.0), axis=-1, keepdims=True)
    rank2 = jnp.sum(jnp.where(hit2, earlier, 0.0), axis=-1, keepdims=True)
    new_carry = carry + jnp.sum(oh, axis=0, keepdims=True)
    carry_ref[...] = jnp.broadcast_to(new_carry, carry_ref.shape)
    cnt_ref[...] = jnp.broadcast_to(new_carry, cnt_ref.shape)

    vals = (i1 - NG, i2 - NG, w1, w2, rank1, rank2)
    info = jnp.zeros((tm, LANES), f32)
    for k, v in enumerate(vals):
        info = jnp.where(lane == k, v, info)
    info_ref[...] = info


def _route(logits, *, NG, EPG):
    T = logits.shape[0]
    tm = min(512, T)
    return pl.pallas_call(
        functools.partial(_route_kernel, NG=NG, EPG=EPG),
        grid=(T // tm,),
        in_specs=[pl.BlockSpec((tm, LANES), lambda i: (i, 0))],
        out_specs=[pl.BlockSpec((tm, LANES), lambda i: (i, 0)),
                   pl.BlockSpec((SUBLANES, LANES), lambda i: (0, 0))],
        out_shape=[jax.ShapeDtypeStruct((T, LANES), f32), jax.ShapeDtypeStruct((SUBLANES, LANES), f32)],
        scratch_shapes=[pltpu.VMEM((SUBLANES, LANES), f32)],
        compiler_params=_params(("arbitrary",), 32),
        name="route",
    )(logits)


def _dest_kernel(info_ref, ps_ref, dest_ref):
    tm = info_ref.shape[0]
    info = info_ref[...]
    lane = lax.broadcasted_iota(jnp.int32, (tm, LANES), 1)
    lanef = lane.astype(f32)
    ps = ps_ref[...]
    d1 = jnp.sum(jnp.where(lanef == info[:, 0:1], ps, 0.0), axis=-1, keepdims=True) + info[:, 4:5]
    d2 = jnp.sum(jnp.where(lanef == info[:, 1:2], ps, 0.0), axis=-1, keepdims=True) + info[:, 5:6]
    dest_ref[...] = jnp.where(lane == 0, d1, jnp.where(lane == 1, d2, 0.0)).astype(jnp.int32)


def _dest(info, pstart_row):
    T = info.shape[0]
    tm = min(512, T)
    return pl.pallas_call(
        _dest_kernel,
        grid=(T // tm,),
        in_specs=[pl.BlockSpec((tm, LANES), lambda i: (i, 0)), pl.BlockSpec((1, LANES), lambda i: (0, 0))],
        out_specs=pl.BlockSpec((tm, LANES), lambda i: (i, 0)),
        out_shape=jax.ShapeDtypeStruct((T, LANES), jnp.int32),
        compiler_params=_params(("parallel",), 32),
        name="dest",
    )(info, pstart_row)


def _row_copy(src, si, dst, di, sem, n=1):
    return pltpu.make_async_copy(src.at[pl.ds(si, n)], dst.at[pl.ds(di, n)], sem)


def _fill_rows(seg_ref, zero_ref, xs_ref, sem, *, wait):
    zrows = zero_ref.shape[0]

    def act(start, n):
        cp = _row_copy(zero_ref, 0, xs_ref, start, sem, n)
        cp.wait() if wait else cp.start()

    def aligned(start, n):
        act(pl.multiple_of(start, SUBLANES), n)

    def per_segment(e, carry):
        lo = seg_ref[0, e]
        hi = seg_ref[1, e]
        lo_al = jnp.minimum((lo + SUBLANES - 1) // SUBLANES * SUBLANES, hi)

        def single(r, cc):
            act(r, 1)
            return cc

        lax.fori_loop(lo, lo_al, single, 0)
        count = hi - lo_al
        nfull = count // zrows

        def full(c, cc):
            aligned(lo_al + c * zrows, zrows)
            return cc

        lax.fori_loop(0, nfull, full, 0)
        pos = lo_al + nfull * zrows
        rem = count - nfull * zrows
        n = zrows // 2
        while n >= SUBLANES:
            take = (rem & n) != 0
            pl.when(take)(functools.partial(aligned, pos, n))
            pos = pos + jnp.where(take, n, 0)
            n //= 2
        return carry

    lax.fori_loop(0, seg_ref.shape[1], per_segment, 0)


def _dispatch_kernel(seg_ref, dest_ref, hn_ref, xs_ref, zero_ref, sem, zsem):
    tm = hn_ref.shape[0]

    @pl.when(pl.program_id(0) == 0)
    def _():
        zero_ref[...] = jnp.zeros_like(zero_ref)
        _fill_rows(seg_ref, zero_ref, xs_ref, zsem, wait=False)
        _fill_rows(seg_ref, zero_ref, xs_ref, zsem, wait=True)

    for r in range(tm):
        for k in range(TOP_K):
            _row_copy(hn_ref, r, xs_ref, dest_ref[0, k, r], sem).start(priority=k % 2)

    def drain(r, carry):
        for k in range(TOP_K):
            _row_copy(hn_ref, 0, xs_ref, 0, sem).wait()
        return carry

    lax.fori_loop(0, tm, drain, 0, unroll=8)


def _dispatch(seg, dest_blocks, hn2, n_rows):
    T, D = hn2.shape
    tm = dest_blocks.shape[2]
    grid_spec = pltpu.PrefetchScalarGridSpec(
        num_scalar_prefetch=1,
        grid=(T // tm,),
        in_specs=[pl.BlockSpec((1, TOP_K, tm), lambda i, seg: (i, 0, 0), memory_space=pltpu.SMEM),
                  pl.BlockSpec((tm, D), lambda i, seg: (i, 0))],
        out_specs=pl.BlockSpec(memory_space=pl.ANY),
        scratch_shapes=[pltpu.VMEM((EXPERT_BLOCK // 2, D), hn2.dtype), pltpu.SemaphoreType.DMA(()),
                        pltpu.SemaphoreType.DMA(())],
    )
    return pl.pallas_call(
        _dispatch_kernel,
        grid_spec=grid_spec,
        out_shape=jax.ShapeDtypeStruct((n_rows, D), hn2.dtype),
        compiler_params=_params(("arbitrary",), 32),
        name="dispatch",
    )(seg, dest_blocks, hn2)


def _experts_kernel(be_ref, nu_ref, re_ref, xs_ref, w1_ref, w3_ref, w2_ref, ys_ref, st1, st3, st2, wb1, wb3, wb2, sems):
    i = pl.program_id(0)
    e = be_ref[i]
    prev = be_ref[jnp.maximum(i - 1, 0)]
    run_end = re_ref[e]

    def fetch(ex):
        return [pltpu.make_async_copy(w1_ref.at[ex], st1, sems.at[0]),
                pltpu.make_async_copy(w3_ref.at[ex], st3, sems.at[1]),
                pltpu.make_async_copy(w2_ref.at[ex], st2, sems.at[2])]

    @pl.when(i == 0)
    def _():
        for cp in fetch(e):
            cp.start()

    @pl.when(jnp.logical_or(i == 0, jnp.logical_and(e != prev, i < nu_ref[0])))
    def _():
        for cp in fetch(e):
            cp.wait()
        wb1[...] = st1[...].astype(bf16)
        wb3[...] = st3[...].astype(bf16)
        wb2[...] = st2[...].astype(bf16)
        @pl.when(run_end < nu_ref[0])
        def _():
            for cp in fetch(be_ref[jnp.minimum(run_end, pl.num_programs(0) - 1)]):
                cp.start()

    @pl.when(i < nu_ref[0])
    def _():
        x = xs_ref[...].astype(bf16)
        h1 = jnp.dot(x, wb1[...], preferred_element_type=f32)
        h3 = jnp.dot(x, wb3[...], preferred_element_type=f32)
        a = (h1 * jax.nn.sigmoid(h1) * h3).astype(bf16)
        ys_ref[...] = jnp.dot(a, wb2[...], preferred_element_type=f32)

    @pl.when(i >= nu_ref[0])
    def _():
        ys_ref[...] = jnp.zeros_like(ys_ref)


def _experts(blk_e, n_used, run_ends, xs, w1, w3, w2):
    P, D = xs.shape
    NE, _, De = w1.shape
    BM = EXPERT_BLOCK
    grid_spec = pltpu.PrefetchScalarGridSpec(
        num_scalar_prefetch=3,
        grid=(P // BM,),
        in_specs=[pl.BlockSpec((BM, D), lambda i, be, nu, nx: (jnp.minimum(i, nu[0] - 1), 0)),
                  pl.BlockSpec(memory_space=pl.ANY), pl.BlockSpec(memory_space=pl.ANY),
                  pl.BlockSpec(memory_space=pl.ANY)],
        out_specs=pl.BlockSpec((BM, D), lambda i, be, nu, nx: (i, 0)),
        scratch_shapes=[pltpu.VMEM((D, De), w1.dtype), pltpu.VMEM((D, De), w3.dtype), pltpu.VMEM((De, D), w2.dtype),
                        pltpu.VMEM((D, De), bf16), pltpu.VMEM((D, De), bf16), pltpu.VMEM((De, D), bf16),
                        pltpu.SemaphoreType.DMA((3,))],
    )
    return pl.pallas_call(
        _experts_kernel,
        grid_spec=grid_spec,
        out_shape=jax.ShapeDtypeStruct((P, D), f32),
        compiler_params=_params(("arbitrary",), 48),
        name="experts",
    )(blk_e, n_used, run_ends, xs, w1, w3, w2)


GATHER_SLOTS = 3


def _final_kernel(dcur_ref, dnx1_ref, dnx2_ref, x1_ref, p_ref, info_ref, ys_ref, wup_ref, wg_ref, gple_ref, gfin_ref,
                  out_ref, buf_ref, x2_ref, sems, *, final_norm):
    tm = x1_ref.shape[0]
    i = pl.program_id(0)
    slot = lax.rem(i, GATHER_SLOTS)
    slot1 = lax.rem(i + 1, GATHER_SLOTS)
    slot2 = lax.rem(i + 2, GATHER_SLOTS)

    def gather(dref, r, s):
        return [_row_copy(ys_ref, dref[0, k, r], buf_ref.at[s, k], r, sems.at[s]) for k in range(TOP_K)]

    def drain(s):
        def body(r, carry):
            for cp in gather(dcur_ref, 0, s):
                cp.wait()
            return carry
        lax.fori_loop(0, tm, body, 0, unroll=8)

    @pl.when(i == 0)
    def _():
        def body(r, carry):
            for cp in gather(dcur_ref, r, 0) + gather(dnx1_ref, r, 1):
                cp.start()
            return carry
        lax.fori_loop(0, tm, body, 0, unroll=8)

    drain(slot)
    info = info_ref[...]
    x2_ref[...] = x1_ref[...] + buf_ref[slot, 0] * info[:, 2:3] + buf_ref[slot, 1] * info[:, 3:4]

    for r in range(tm):
        for k, cp in enumerate(gather(dnx2_ref, r, slot2)):
            cp.start(priority=k % 2)

    ple = jnp.dot(p_ref[...].astype(bf16), wup_ref[...], preferred_element_type=f32)
    x2 = x2_ref[...]
    hn3 = _rms(x2, gple_ref[...]).astype(bf16)
    gate = jax.nn.sigmoid(jnp.dot(hn3, wg_ref[...], preferred_element_type=f32))
    x3 = x2 + gate * ple
    out_ref[...] = _rms(x3, gfin_ref[...]) if final_norm else x3

    @pl.when(i == pl.num_programs(0) - 1)
    def _():
        drain(slot1)
        drain(slot2)


def _final(dest_blocks, x1, p2d, info, ys, w_up, w_gate, g_ple, g_final, *, final_norm):
    T, D = x1.shape
    PD = p2d.shape[1]
    tm = dest_blocks.shape[2]
    nt = T // tm
    row = lambda i: (i, 0)

    def dest_spec(ahead):
        return pl.BlockSpec((1, TOP_K, tm), lambda i: (jnp.minimum(i + ahead, nt - 1), 0, 0),
                            memory_space=pltpu.SMEM)

    return pl.pallas_call(
        functools.partial(_final_kernel, final_norm=final_norm),
        grid=(nt,),
        in_specs=[dest_spec(a) for a in range(GATHER_SLOTS)] + [
                  pl.BlockSpec((tm, D), row),
                  pl.BlockSpec((tm, PD), row),
                  pl.BlockSpec((tm, LANES), row),
                  pl.BlockSpec(memory_space=pl.ANY),
                  _const_spec((PD, D)), _const_spec((D, D)), _const_spec((1, D)), _const_spec((1, D))],
        out_specs=pl.BlockSpec((tm, D), row),
        out_shape=jax.ShapeDtypeStruct((T, D), f32),
        scratch_shapes=[pltpu.VMEM((GATHER_SLOTS, TOP_K, tm, D), ys.dtype), pltpu.VMEM((tm, D), f32),
                        pltpu.SemaphoreType.DMA((GATHER_SLOTS,))],
        compiler_params=_params(("arbitrary",), 48),
        name="final",
    )(dest_blocks, dest_blocks, dest_blocks, x1, p2d, info, ys, w_up, w_gate, g_ple, g_final)


def _layer(x2d, p2d, g_mix, w_in, conv_w, conv_b, b_gate, gn_m, ln_g, ln_b, w_s, b_s, w_bm, w_bg, w_out,
           g_ffn, w_rg, b_rg, w_re, b_re, w1, w3, w2, g_ple, w_ple_up, w_ple_gate, g_out, *, B, S, final_norm):
    T, D = x2d.shape
    H = b_gate.shape[0] // 2
    W = gn_m.shape[0]
    GW = ln_g.shape[0]
    NG = w_rg.shape[1]
    NE = w_re.shape[1]
    EPG = NE // NG
    BM = EXPERT_BLOCK
    KW = conv_w.shape[0]
    tn = PROJ_COLS
    row2 = lambda v: v.reshape(1, -1).astype(f32)

    n_if = 2 * H
    c_if = 4 * W
    c_uv = c_if + n_if
    w_bf = w_in.astype(bf16)
    w_tail = w_bf[:, c_uv:]
    w_if = jnp.pad(w_bf[:, c_if:c_uv], ((0, 0), (0, LANES - n_if)))
    bg_pad = jnp.concatenate([b_gate.astype(f32), jnp.zeros((LANES - n_if,), f32)]).reshape(1, LANES)

    tm = min(PROJ_ROWS, S)
    hspec = lambda shape: pl.BlockSpec(shape, lambda i, j: (0, j))
    qk, hn = pl.pallas_call(
        functools.partial(_proj_qk_kernel, tiles_per_seq=S // tm, KW=KW, k_scale=(W // H) ** -0.5),
        grid=(T // tm, 2 * W // tn),
        in_specs=[pl.BlockSpec((tm, D), lambda i, j: (i, 0)), pl.BlockSpec((1, D), lambda i, j: (0, 0)),
                  hspec((D, tn)), hspec((KW, tn)), hspec((1, tn))],
        out_specs=[pl.BlockSpec((tm, tn), lambda i, j: (i, j)), pl.BlockSpec((tm, D), lambda i, j: (i, 0))],
        out_shape=[jax.ShapeDtypeStruct((T, 2 * W), bf16), jax.ShapeDtypeStruct((T, D), bf16)],
        scratch_shapes=[pltpu.VMEM((2 * W // tn, SUBLANES, tn), f32)],
        compiler_params=_params(("arbitrary", "arbitrary"), 52),
        name="proj_qk",
    )(x2d, row2(g_mix), w_bf, conv_w.astype(f32), row2(conv_b))
    v, pif = _proj("proj_v", _proj_v_kernel, hn, w_bf, 2 * W // tn, W // tn, tm=tm,
                   extra=(w_if,), extra_specs=(pl.BlockSpec((D, LANES), lambda i, j: (0, 0)),),
                   extra_out=(jax.ShapeDtypeStruct((T, LANES), f32),),
                   extra_out_specs=(pl.BlockSpec((tm, LANES), lambda i, j: (i, 0)),))
    so = _proj("proj_o", functools.partial(_proj_act_kernel, act=jax.nn.sigmoid), hn, w_bf, 3 * W // tn, W // tn, tm=tm)
    ug = _proj("proj_u", functools.partial(_proj_act_kernel, act=jax.nn.gelu), hn, w_tail, 0, GW // tn, tm=tm)
    vn = _proj("proj_vn", _proj_vn_kernel, hn, w_tail, GW // tn, 1, tm=tm,
               extra=(row2(ln_g), row2(ln_b)), extra_specs=(hspec((1, tn)), hspec((1, tn))),
               scratch=(pltpu.VMEM((tm, tn), f32),))
    gates = _proj("proj_gates", functools.partial(_proj_act_kernel, act=jax.nn.sigmoid), hn, w_tail,
                  2 * GW // tn, 2 * D // tn, tm=tm)

    h_m = _mlstm(qk, v, so, pif, bg_pad, row2(gn_m), B=B, S=S, H=H)

    w_r = jnp.concatenate([w_rg, w_re, jnp.zeros((D, LANES - NG - NE), w_rg.dtype)], axis=1).astype(bf16)
    b_r = jnp.concatenate([b_rg.astype(f32), b_re.astype(f32), jnp.zeros((LANES - NG - NE,), f32)]).reshape(1, LANES)
    x1, hn2, logits = _merge(h_m, ug, vn, gates, x2d, jnp.tril(w_s).astype(bf16), jnp.transpose(b_s).astype(f32),
                             w_bm.astype(bf16), w_bg.astype(bf16), w_out.astype(bf16), row2(g_ffn), w_r, b_r)

    info, cnt = _route(logits, NG=NG, EPG=EPG)
    counts = cnt[0, NG:NG + NE].astype(jnp.int32)
    padded = (counts + BM - 1) // BM * BM
    pends = jnp.cumsum(padded)
    pstarts = pends - padded
    NB = (T * TOP_K) // BM + NE
    blk_start = jnp.arange(NB, dtype=jnp.int32) * BM
    blk_e = jnp.minimum(jnp.sum((pends[None, :] <= blk_start[:, None]).astype(jnp.int32), axis=1), NE - 1)
    n_used = (pends[-1:] // BM).astype(jnp.int32)
    run_ends = (pends // BM).astype(jnp.int32)
    ps_row = jnp.zeros((1, LANES), f32).at[0, :NE].set(pstarts.astype(f32))
    seg = jnp.stack([jnp.concatenate([pstarts + counts, pends[-1:]]),
                     jnp.concatenate([pends, jnp.full((1,), NB * BM, jnp.int32)])]).astype(jnp.int32)

    dest = _dest(info, ps_row)
    tmd = min(256, T)
    dest_blocks = dest[:, :TOP_K].reshape(T // tmd, tmd, TOP_K).transpose(0, 2, 1)

    xs = _dispatch(seg, dest_blocks, hn2, NB * BM)
    ys = _experts(blk_e, n_used, run_ends, xs, w1, w3, w2)
    return _final(dest_blocks, x1, p2d, info, ys, w_ple_up.astype(bf16), w_ple_gate.astype(bf16),
                  row2(g_ple), row2(g_out), final_norm=final_norm)


def kernel(x, p, g_mix, w_in, conv_w, conv_b, b_gate, gn_m, ln_g, ln_b, w_s, b_s, w_bm, w_bg, w_out, g_ffn, w_rg, b_rg, w_re, b_re, w1, w3, w2, g_ple, w_ple_up, w_ple_gate, g_final):
    B, S, D = x.shape
    depth = w_in.shape[0]
    x2d = x.reshape(B * S, D)
    for i in range(depth):
        last = i == depth - 1
        x2d = _layer(x2d, p[i].reshape(B * S, -1), g_mix[i], w_in[i], conv_w[i], conv_b[i], b_gate[i], gn_m[i],
                     ln_g[i], ln_b[i], w_s[i], b_s[i], w_bm[i], w_bg[i], w_out[i], g_ffn[i], w_rg[i], b_rg[i],
                     w_re[i], b_re[i], w1[i], w3[i], w2[i], g_ple[i], w_ple_up[i], w_ple_gate[i], g_final,
                     B=B, S=S, final_norm=last)
    return x2d.reshape(B, S, D)
```

```python
import functools

import jax
import jax.numpy as jnp
from jax import lax
from jax.experimental import pallas as pl
from jax.experimental.pallas import tpu as pltpu

EPS = 1e-6
M_CHUNK = 128
TOP_K = 2
LANES = 128
SUBLANES = 8
MXU_DIM = 256
EXPERT_BLOCK = 256
HEAD_GROUP = 4
PROJ_ROWS = 1024
PROJ_COLS = 1024
MIB = 1024 * 1024

f32 = jnp.float32
bf16 = jnp.bfloat16


def _params(semantics, vmem_mib):
    return pltpu.CompilerParams(dimension_semantics=semantics, vmem_limit_bytes=vmem_mib * MIB)


def _const_spec(shape):
    nd = len(shape)
    return pl.BlockSpec(shape, lambda *_: (0,) * nd, pipeline_mode=pl.Buffered(1))


def _rms(x, g):
    ms = jnp.mean(x * x, axis=-1, keepdims=True)
    return x * lax.rsqrt(ms + EPS) * g


def _chunk_cols(n):
    return slice(n * MXU_DIM, (n + 1) * MXU_DIM)


def _dot_chunk(hn_ref, w_ref, n):
    return jnp.dot(hn_ref[...], w_ref[:, _chunk_cols(n)], preferred_element_type=f32)


def _proj_act_kernel(hn_ref, w_ref, o_ref, *, act):
    for n in range(o_ref.shape[1] // MXU_DIM):
        o_ref[:, _chunk_cols(n)] = act(_dot_chunk(hn_ref, w_ref, n)).astype(o_ref.dtype)


def _proj_v_kernel(hn_ref, w_ref, wif_ref, o_ref, oif_ref):
    _proj_act_kernel(hn_ref, w_ref, o_ref, act=lambda t: t)
    oif_ref[...] = jnp.dot(hn_ref[...], wif_ref[...], preferred_element_type=f32)


def _proj_qk_kernel(x_ref, g_ref, w_ref, cw_ref, cb_ref, o_ref, hn_ref, carry_ref, *, tiles_per_seq, KW, k_scale):
    i = pl.program_id(0)
    j = pl.program_id(1)
    tm = hn_ref.shape[0]
    scale = jnp.where(j == 1, k_scale, 1.0).astype(f32)

    @pl.when(j == 0)
    def _():
        hn_ref[...] = _rms(x_ref[...], g_ref[...]).astype(hn_ref.dtype)

    @pl.when(lax.rem(i, tiles_per_seq) == 0)
    def _():
        carry_ref[j] = jnp.zeros(carry_ref.shape[1:], f32)

    for n in range(o_ref.shape[1] // MXU_DIM):
        cols = _chunk_cols(n)
        acc = _dot_chunk(hn_ref, w_ref, n)
        prev = carry_ref[j, :, cols]
        carry_ref[j, :, cols] = acc[tm - SUBLANES:, :]
        ext = jnp.concatenate([prev, acc], axis=0)
        y = cb_ref[:, cols] + cw_ref[KW - 1:KW, cols] * acc
        for s in range(1, KW):
            y = y + cw_ref[KW - 1 - s:KW - s, cols] * pltpu.roll(ext, s, axis=0)[SUBLANES:, :]
        o_ref[:, cols] = (y * jax.nn.sigmoid(y) * scale).astype(o_ref.dtype)


def _proj_vn_kernel(hn_ref, w_ref, lng_ref, lnb_ref, o_ref, g_ref):
    nch = o_ref.shape[1] // MXU_DIM
    width = o_ref.shape[1]
    total = 0.0
    for n in range(nch):
        g = jax.nn.gelu(_dot_chunk(hn_ref, w_ref, n))
        g_ref[:, _chunk_cols(n)] = g
        total = total + jnp.sum(g, axis=-1, keepdims=True)
    mu = total / width
    sq = 0.0
    for n in range(nch):
        xc = g_ref[:, _chunk_cols(n)] - mu
        sq = sq + jnp.sum(xc * xc, axis=-1, keepdims=True)
    inv = lax.rsqrt(sq / width + EPS)
    for n in range(nch):
        cols = _chunk_cols(n)
        o_ref[:, cols] = ((g_ref[:, cols] - mu) * inv * lng_ref[:, cols] + lnb_ref[:, cols]).astype(o_ref.dtype)


def _proj(name, body, hn, w, col0, ncol, *, extra=(), extra_specs=(), extra_out=(), extra_out_specs=(), scratch=(),
          tm):
    T, D = hn.shape
    tn = PROJ_COLS
    outs = pl.pallas_call(
        body,
        grid=(T // tm, ncol),
        in_specs=[pl.BlockSpec((tm, D), lambda i, j: (i, 0)),
                  pl.BlockSpec((D, tn), lambda i, j: (0, col0 + j))] + list(extra_specs),
        out_specs=[pl.BlockSpec((tm, tn), lambda i, j: (i, j))] + list(extra_out_specs),
        out_shape=[jax.ShapeDtypeStruct((T, ncol * tn), bf16)] + list(extra_out),
        scratch_shapes=list(scratch),
        compiler_params=_params(("arbitrary", "arbitrary"), 56),
        name=name,
    )(hn, w, *extra)
    return outs if extra_out else outs[0]


def _mlstm_kernel(q_ref, k_ref, v_ref, so_ref, if_ref, bg_ref, gn_ref, out_ref, c_ref, m_ref, *, H, Dh):
    @pl.when(pl.program_id(1) == 0)
    def _():
        c_ref[...] = jnp.zeros_like(c_ref)
        m_ref[...] = jnp.zeros_like(m_ref)

    for bb in range(q_ref.shape[0]):
        _mlstm_chunk(q_ref.at[bb], k_ref.at[bb], v_ref.at[bb], so_ref.at[bb], if_ref.at[bb], bg_ref, gn_ref,
                     out_ref.at[bb], c_ref.at[bb], m_ref.at[bb], H=H, Dh=Dh)


def _mlstm_chunk(q_ref, k_ref, v_ref, so_ref, if_ref, bg_ref, gn_ref, out_ref, c_ref, m_ref, *, H, Dh):
    L = M_CHUNK
    gp = if_ref[...] + bg_ref[...]
    lane = lax.broadcasted_iota(jnp.int32, (L, LANES), 1)
    lf = jnp.minimum(gp, 0.0) - jnp.log1p(jnp.exp(-jnp.abs(gp)))
    z = jnp.where(lane < H, gp, lf)
    row = lax.broadcasted_iota(jnp.int32, (L, L), 0)
    col = lax.broadcasted_iota(jnp.int32, (L, L), 1)
    causal = row >= col
    bc = jnp.dot(causal.astype(f32), z, precision=lax.Precision.HIGHEST, preferred_element_type=f32)
    zt = z.T
    bct = bc.T

    ones = jnp.ones((L, LANES), bf16)
    for heads in [tuple(range(h0, min(h0 + HEAD_GROUP, H))) for h0 in range(0, H, HEAD_GROUP)]:
        per = lambda f: {h: f(h) for h in heads}
        sl = per(lambda h: slice(h * Dh, (h + 1) * Dh))
        qb = per(lambda h: q_ref[:, sl[h]])
        kb = per(lambda h: k_ref[:, sl[h]])
        va = per(lambda h: jnp.concatenate([v_ref[:, sl[h]], ones], axis=1))
        b_col = per(lambda h: bc[:, H + h:H + h + 1])
        g = per(lambda h: b_col[h][L - 1:L, :])
        m_prev = per(lambda h: m_ref[h, 0:1, 0:1])

        a_col = per(lambda h: g[h] - b_col[h] + z[:, h:h + 1])
        m_loc = per(lambda h: jnp.max(a_col[h], axis=0, keepdims=True))
        w_loc = per(lambda h: jnp.exp(a_col[h] - m_loc[h]))
        log_d = per(lambda h: jnp.where(causal, b_col[h] - bct[H + h:H + h + 1, :] + zt[h:h + 1, :], -jnp.inf))
        log_inter = per(lambda h: b_col[h] + m_prev[h])
        mt = per(lambda h: jnp.maximum(log_inter[h], jnp.max(log_d[h], axis=-1, keepdims=True)))
        d = per(lambda h: jnp.exp(log_d[h] - mt[h]))
        s_inter = per(lambda h: jnp.exp(log_inter[h] - mt[h]))

        s = per(lambda h: lax.dot_general(qb[h], kb[h], (((1,), (1,)), ((), ())), preferred_element_type=f32) * d[h])
        inter = per(lambda h: jnp.dot(qb[h], c_ref[h].astype(bf16), preferred_element_type=f32))
        nd = per(lambda h: jnp.dot(s[h].astype(bf16), va[h], preferred_element_type=f32)
                 + s_inter[h] * inter[h])
        hh = per(lambda h: nd[h][:, :Dh] / jnp.maximum(jnp.abs(nd[h][:, Dh:Dh + 1]), jnp.exp(-mt[h])))

        m_new = per(lambda h: jnp.maximum(g[h] + m_prev[h], m_loc[h]))
        kv = per(lambda h: jnp.dot((kb[h].astype(f32) * w_loc[h]).T.astype(bf16), va[h], preferred_element_type=f32))
        for h in heads:
            c_ref[h] = jnp.exp(g[h] + m_prev[h] - m_new[h]) * c_ref[h] + jnp.exp(m_loc[h] - m_new[h]) * kv[h]
            m_ref[h] = jnp.broadcast_to(m_new[h], m_ref.shape[1:])

        mu = per(lambda h: jnp.mean(hh[h], axis=-1, keepdims=True))
        xc = per(lambda h: hh[h] - mu[h])
        var = per(lambda h: jnp.mean(xc[h] * xc[h], axis=-1, keepdims=True))
        for h in heads:
            hn = xc[h] * lax.rsqrt(var[h] + EPS) * gn_ref[:, sl[h]]
            out_ref[:, sl[h]] = (so_ref[:, sl[h]].astype(f32) * hn).astype(out_ref.dtype)


def _mlstm(qk, v, so, pif, bg_pad, gn_m, *, B, S, H):
    T, W = v.shape
    L = M_CHUNK
    Dh = W // H
    BB = 1
    seq = lambda a: a.reshape(B, S, a.shape[-1])
    blk = lambda width, cb: pl.BlockSpec((BB, L, width), lambda b, c: (b, c, cb))
    out = pl.pallas_call(
        functools.partial(_mlstm_kernel, H=H, Dh=Dh),
        grid=(B // BB, S // L),
        in_specs=[blk(W, 0), blk(W, 1), blk(W, 0), blk(W, 0), blk(LANES, 0),
                  pl.BlockSpec((1, LANES), lambda b, c: (0, 0)),
                  pl.BlockSpec((1, W), lambda b, c: (0, 0))],
        out_specs=blk(W, 0),
        out_shape=jax.ShapeDtypeStruct((B, S, W), bf16),
        scratch_shapes=[pltpu.VMEM((BB, H, Dh, Dh + LANES), f32),
                        pltpu.VMEM((BB, H, SUBLANES, LANES), f32)],
        compiler_params=_params(("parallel", "arbitrary"), 32),
        name="mlstm",
    )(seq(qk), seq(qk), seq(v), seq(so), seq(pif), bg_pad, gn_m)
    return out.reshape(T, W)


def _merge_kernel(hm_ref, ug_ref, vn_ref, gm_ref, gg_ref, x_ref, ws_ref, bst_ref, wbm_ref, wbg_ref, wout_ref,
                  gffn_ref, wr_ref, br_ref, x1_ref, hn2_ref, lg_ref, hg_ref):
    tm = x_ref.shape[0]
    G, L, _ = ws_ref.shape
    Gd = ug_ref.shape[1] // G
    for c in range(tm // L):
        rows = slice(c * L, (c + 1) * L)
        for g in range(G):
            cols = slice(g * Gd, (g + 1) * Gd)
            mixed = jnp.dot(ws_ref[g], vn_ref[rows, cols], preferred_element_type=f32) + bst_ref[:, g:g + 1]
            hg_ref[rows, cols] = (ug_ref[rows, cols].astype(f32) * mixed).astype(hg_ref.dtype)

    a = jnp.dot(hm_ref[...], wbm_ref[...], preferred_element_type=f32)
    b = jnp.dot(hg_ref[...], wbg_ref[...], preferred_element_type=f32)
    merged = gm_ref[...].astype(f32) * a + gg_ref[...].astype(f32) * b
    x1 = x_ref[...] + jnp.dot(merged.astype(bf16), wout_ref[...], preferred_element_type=f32)
    x1_ref[...] = x1
    hn2 = _rms(x1, gffn_ref[...])
    hn2_ref[...] = hn2
    lg_ref[...] = jnp.dot(hn2.astype(bf16), wr_ref[...], preferred_element_type=f32) + br_ref[...]


def _merge(h_m, ug, vn, gates, x2d, ws_tril, b_st, w_bm, w_bg, w_out, g_ffn, w_r, b_r):
    T, D = x2d.shape
    W = h_m.shape[1]
    GW = ug.shape[1]
    G, L, _ = ws_tril.shape
    tm = min(256, T)
    row = lambda i: (i, 0)
    return pl.pallas_call(
        _merge_kernel,
        grid=(T // tm,),
        in_specs=[pl.BlockSpec((tm, W), row),
                  pl.BlockSpec((tm, GW), row),
                  pl.BlockSpec((tm, GW), row),
                  pl.BlockSpec((tm, D), lambda i: (i, 0)),
                  pl.BlockSpec((tm, D), lambda i: (i, 1)),
                  pl.BlockSpec((tm, D), row),
                  _const_spec((G, L, L)), _const_spec((L, G)),
                  _const_spec((W, D)), _const_spec((GW, D)), _const_spec((D, D)),
                  _const_spec((1, D)), _const_spec((D, LANES)), _const_spec((1, LANES))],
        out_specs=[pl.BlockSpec((tm, D), row), pl.BlockSpec((tm, D), row), pl.BlockSpec((tm, LANES), row)],
        out_shape=[jax.ShapeDtypeStruct((T, D), f32), jax.ShapeDtypeStruct((T, D), f32),
                   jax.ShapeDtypeStruct((T, LANES), f32)],
        scratch_shapes=[pltpu.VMEM((tm, GW), bf16)],
        compiler_params=_params(("parallel",), 48),
        name="merge",
    )(h_m, ug, vn, gates, gates, x2d, ws_tril, b_st, w_bm, w_bg, w_out, g_ffn, w_r, b_r)


def _route_kernel(lg_ref, info_ref, cnt_ref, carry_ref, *, NG, EPG):
    tm = lg_ref.shape[0]

    @pl.when(pl.program_id(0) == 0)
    def _():
        carry_ref[...] = jnp.zeros_like(carry_ref)

    lg = lg_ref[...]
    lane = lax.broadcasted_iota(jnp.int32, (tm, LANES), 1)
    lanef = lane.astype(f32)
    big = float(LANES)

    def first_max(vals):
        mx = jnp.max(vals, axis=-1, keepdims=True)
        idx = jnp.min(jnp.where(vals == mx, lanef, big), axis=-1, keepdims=True)
        return mx, idx

    is_g = lane < NG
    gmax, grp = first_max(jnp.where(is_g, lg, -jnp.inf))
    p_grp = 1.0 / jnp.sum(jnp.where(is_g, jnp.exp(lg - gmax), 0.0), axis=-1, keepdims=True)
    lo = NG + grp * EPG
    el = jnp.where(jnp.logical_and(lanef >= lo, lanef < lo + EPG), lg, -jnp.inf)
    v1, i1 = first_max(el)
    v2, i2 = first_max(jnp.where(lanef == i1, -jnp.inf, el))
    t = jnp.exp(v2 - v1)
    w1 = p_grp / (1.0 + t)
    w2 = p_grp * t / (1.0 + t)

    hit1 = lanef == i1
    hit2 = lanef == i2
    oh = jnp.where(jnp.logical_or(hit1, hit2), 1.0, 0.0)
    r = lax.broadcasted_iota(jnp.int32, (tm, tm), 0)
    c = lax.broadcasted_iota(jnp.int32, (tm, tm), 1)
    before = jnp.where(r > c, 1.0, 0.0).astype(bf16)
    carry = carry_ref[0:1, :]
    earlier = jnp.dot(before, oh.astype(bf16), preferred_element_type=f32) + carry
    rank1 = jnp.sum(jnp.where(hit1, earlier, 0.0), axis=-1, keepdims=True)
    rank2 = jnp.sum(jnp.where(hit2, earlier, 0.0), axis=-1, keepdims=True)
    new_carry = carry + jnp.sum(oh, axis=0, keepdims=True)
    carry_ref[...] = jnp.broadcast_to(new_carry, carry_ref.shape)
    cnt_ref[...] = jnp.broadcast_to(new_carry, cnt_ref.shape)

    vals = (i1 - NG, i2 - NG, w1, w2, rank1, rank2)
    info = jnp.zeros((tm, LANES), f32)
    for k, v in enumerate(vals):
        info = jnp.where(lane == k, v, info)
    info_ref[...] = info


def _route(logits, *, NG, EPG):
    T = logits.shape[0]
    tm = min(512, T)
    return pl.pallas_call(
        functools.partial(_route_kernel, NG=NG, EPG=EPG),
        grid=(T // tm,),
        in_specs=[pl.BlockSpec((tm, LANES), lambda i: (i, 0))],
        out_specs=[pl.BlockSpec((tm, LANES), lambda i: (i, 0)),
                   pl.BlockSpec((SUBLANES, LANES), lambda i: (0, 0))],
        out_shape=[jax.ShapeDtypeStruct((T, LANES), f32), jax.ShapeDtypeStruct((SUBLANES, LANES), f32)],
        scratch_shapes=[pltpu.VMEM((SUBLANES, LANES), f32)],
        compiler_params=_params(("arbitrary",), 32),
        name="route",
    )(logits)


def _dest_kernel(info_ref, ps_ref, dest_ref):
    tm = info_ref.shape[0]
    info = info_ref[...]
    lane = lax.broadcasted_iota(jnp.int32, (tm, LANES), 1)
    lanef = lane.astype(f32)
    ps = ps_ref[...]
    d1 = jnp.sum(jnp.where(lanef == info[:, 0:1], ps, 0.0), axis=-1, keepdims=True) + info[:, 4:5]
    d2 = jnp.sum(jnp.where(lanef == info[:, 1:2], ps, 0.0), axis=-1, keepdims=True) + info[:, 5:6]
    dest_ref[...] = jnp.where(lane == 0, d1, jnp.where(lane == 1, d2, 0.0)).astype(jnp.int32)


def _dest(info, pstart_row):
    T = info.shape[0]
    tm = min(512, T)
    return pl.pallas_call(
        _dest_kernel,
        grid=(T // tm,),
        in_specs=[pl.BlockSpec((tm, LANES), lambda i: (i, 0)), pl.BlockSpec((1, LANES), lambda i: (0, 0))],
        out_specs=pl.BlockSpec((tm, LANES), lambda i: (i, 0)),
        out_shape=jax.ShapeDtypeStruct((T, LANES), jnp.int32),
        compiler_params=_params(("parallel",), 32),
        name="dest",
    )(info, pstart_row)


def _row_copy(src, si, dst, di, sem, n=1):
    return pltpu.make_async_copy(src.at[pl.ds(si, n)], dst.at[pl.ds(di, n)], sem)


def _fill_rows(seg_ref, zero_ref, xs_ref, sem, *, wait):
    zrows = zero_ref.shape[0]

    def act(start, n):
        cp = _row_copy(zero_ref, 0, xs_ref, start, sem, n)
        cp.wait() if wait else cp.start()

    def aligned(start, n):
        act(pl.multiple_of(start, SUBLANES), n)

    def per_segment(e, carry):
        lo = seg_ref[0, e]
        hi = seg_ref[1, e]
        lo_al = jnp.minimum((lo + SUBLANES - 1) // SUBLANES * SUBLANES, hi)

        def single(r, cc):
            act(r, 1)
            return cc

        lax.fori_loop(lo, lo_al, single, 0)
        count = hi - lo_al
        nfull = count // zrows

        def full(c, cc):
            aligned(lo_al + c * zrows, zrows)
            return cc

        lax.fori_loop(0, nfull, full, 0)
        pos = lo_al + nfull * zrows
        rem = count - nfull * zrows
        n = zrows // 2
        while n >= SUBLANES:
            take = (rem & n) != 0
            pl.when(take)(functools.partial(aligned, pos, n))
            pos = pos + jnp.where(take, n, 0)
            n //= 2
        return carry

    lax.fori_loop(0, seg_ref.shape[1], per_segment, 0)


def _dispatch_kernel(seg_ref, dest_ref, hn_ref, xs_ref, zero_ref, sem, zsem):
    tm = hn_ref.shape[0]

    @pl.when(pl.program_id(0) == 0)
    def _():
        zero_ref[...] = jnp.zeros_like(zero_ref)
        _fill_rows(seg_ref, zero_ref, xs_ref, zsem, wait=False)
        _fill_rows(seg_ref, zero_ref, xs_ref, zsem, wait=True)

    for r in range(tm):
        for k in range(TOP_K):
            _row_copy(hn_ref, r, xs_ref, dest_ref[0, k, r], sem).start(priority=k % 2)

    def drain(r, carry):
        for k in range(TOP_K):
            _row_copy(hn_ref, 0, xs_ref, 0, sem).wait()
        return carry

    lax.fori_loop(0, tm, drain, 0, unroll=8)


def _dispatch(seg, dest_blocks, hn2, n_rows):
    T, D = hn2.shape
    tm = dest_blocks.shape[2]
    grid_spec = pltpu.PrefetchScalarGridSpec(
        num_scalar_prefetch=1,
        grid=(T // tm,),
        in_specs=[pl.BlockSpec((1, TOP_K, tm), lambda i, seg: (i, 0, 0), memory_space=pltpu.SMEM),
                  pl.BlockSpec((tm, D), lambda i, seg: (i, 0))],
        out_specs=pl.BlockSpec(memory_space=pl.ANY),
        scratch_shapes=[pltpu.VMEM((EXPERT_BLOCK // 2, D), hn2.dtype), pltpu.SemaphoreType.DMA(()),
                        pltpu.SemaphoreType.DMA(())],
    )
    return pl.pallas_call(
        _dispatch_kernel,
        grid_spec=grid_spec,
        out_shape=jax.ShapeDtypeStruct((n_rows, D), hn2.dtype),
        compiler_params=_params(("arbitrary",), 32),
        name="dispatch",
    )(seg, dest_blocks, hn2)


def _experts_kernel(be_ref, nu_ref, re_ref, xs_ref, w1_ref, w3_ref, w2_ref, ys_ref, st1, st3, st2, wb1, wb3, wb2, sems):
    i = pl.program_id(0)
    e = be_ref[i]
    prev = be_ref[jnp.maximum(i - 1, 0)]
    run_end = re_ref[e]

    def fetch(ex):
        return [pltpu.make_async_copy(w1_ref.at[ex], st1, sems.at[0]),
                pltpu.make_async_copy(w3_ref.at[ex], st3, sems.at[1]),
                pltpu.make_async_copy(w2_ref.at[ex], st2, sems.at[2])]

    @pl.when(i == 0)
    def _():
        for cp in fetch(e):
            cp.start()

    @pl.when(jnp.logical_or(i == 0, jnp.logical_and(e != prev, i < nu_ref[0])))
    def _():
        for cp in fetch(e):
            cp.wait()
        wb1[...] = st1[...].astype(bf16)
        wb3[...] = st3[...].astype(bf16)
        wb2[...] = st2[...].astype(bf16)
        @pl.when(run_end < nu_ref[0])
        def _():
            for cp in fetch(be_ref[jnp.minimum(run_end, pl.num_programs(0) - 1)]):
                cp.start()

    @pl.when(i < nu_ref[0])
    def _():
        x = xs_ref[...].astype(bf16)
        h1 = jnp.dot(x, wb1[...], preferred_element_type=f32)
        h3 = jnp.dot(x, wb3[...], preferred_element_type=f32)
        a = (h1 * jax.nn.sigmoid(h1) * h3).astype(bf16)
        ys_ref[...] = jnp.dot(a, wb2[...], preferred_element_type=f32)

    @pl.when(i >= nu_ref[0])
    def _():
        ys_ref[...] = jnp.zeros_like(ys_ref)


def _experts(blk_e, n_used, run_ends, xs, w1, w3, w2):
    P, D = xs.shape
    NE, _, De = w1.shape
    BM = EXPERT_BLOCK
    grid_spec = pltpu.PrefetchScalarGridSpec(
        num_scalar_prefetch=3,
        grid=(P // BM,),
        in_specs=[pl.BlockSpec((BM, D), lambda i, be, nu, nx: (jnp.minimum(i, nu[0] - 1), 0)),
                  pl.BlockSpec(memory_space=pl.ANY), pl.BlockSpec(memory_space=pl.ANY),
                  pl.BlockSpec(memory_space=pl.ANY)],
        out_specs=pl.BlockSpec((BM, D), lambda i, be, nu, nx: (i, 0)),
        scratch_shapes=[pltpu.VMEM((D, De), w1.dtype), pltpu.VMEM((D, De), w3.dtype), pltpu.VMEM((De, D), w2.dtype),
                        pltpu.VMEM((D, De), bf16), pltpu.VMEM((D, De), bf16), pltpu.VMEM((De, D), bf16),
                        pltpu.SemaphoreType.DMA((3,))],
    )
    return pl.pallas_call(
        _experts_kernel,
        grid_spec=grid_spec,
        out_shape=jax.ShapeDtypeStruct((P, D), f32),
        compiler_params=_params(("arbitrary",), 48),
        name="experts",
    )(blk_e, n_used, run_ends, xs, w1, w3, w2)


GATHER_SLOTS = 3


def _final_kernel(dcur_ref, dnx1_ref, dnx2_ref, x1_ref, p_ref, info_ref, ys_ref, wup_ref, wg_ref, gple_ref, gfin_ref,
                  out_ref, buf_ref, x2_ref, sems, *, final_norm):
    tm = x1_ref.shape[0]
    i = pl.program_id(0)
    slot = lax.rem(i, GATHER_SLOTS)
    slot1 = lax.rem(i + 1, GATHER_SLOTS)
    slot2 = lax.rem(i + 2, GATHER_SLOTS)

    def gather(dref, r, s):
        return [_row_copy(ys_ref, dref[0, k, r], buf_ref.at[s, k], r, sems.at[s]) for k in range(TOP_K)]

    def drain(s):
        def body(r, carry):
            for cp in gather(dcur_ref, 0, s):
                cp.wait()
            return carry
        lax.fori_loop(0, tm, body, 0, unroll=8)

    @pl.when(i == 0)
    def _():
        def body(r, carry):
            for cp in gather(dcur_ref, r, 0) + gather(dnx1_ref, r, 1):
                cp.start()
            return carry
        lax.fori_loop(0, tm, body, 0, unroll=8)

    drain(slot)
    info = info_ref[...]
    x2_ref[...] = x1_ref[...] + buf_ref[slot, 0] * info[:, 2:3] + buf_ref[slot, 1] * info[:, 3:4]

    for r in range(tm):
        for k, cp in enumerate(gather(dnx2_ref, r, slot2)):
            cp.start(priority=k % 2)

    ple = jnp.dot(p_ref[...].astype(bf16), wup_ref[...], preferred_element_type=f32)
    x2 = x2_ref[...]
    hn3 = _rms(x2, gple_ref[...]).astype(bf16)
    gate = jax.nn.sigmoid(jnp.dot(hn3, wg_ref[...], preferred_element_type=f32))
    x3 = x2 + gate * ple
    out_ref[...] = _rms(x3, gfin_ref[...]) if final_norm else x3

    @pl.when(i == pl.num_programs(0) - 1)
    def _():
        drain(slot1)
        drain(slot2)


def _final(dest_blocks, x1, p2d, info, ys, w_up, w_gate, g_ple, g_final, *, final_norm):
    T, D = x1.shape
    PD = p2d.shape[1]
    tm = dest_blocks.shape[2]
    nt = T // tm
    row = lambda i: (i, 0)

    def dest_spec(ahead):
        return pl.BlockSpec((1, TOP_K, tm), lambda i: (jnp.minimum(i + ahead, nt - 1), 0, 0),
                            memory_space=pltpu.SMEM)

    return pl.pallas_call(
        functools.partial(_final_kernel, final_norm=final_norm),
        grid=(nt,),
        in_specs=[dest_spec(a) for a in range(GATHER_SLOTS)] + [
                  pl.BlockSpec((tm, D), row),
                  pl.BlockSpec((tm, PD), row),
                  pl.BlockSpec((tm, LANES), row),
                  pl.BlockSpec(memory_space=pl.ANY),
                  _const_spec((PD, D)), _const_spec((D, D)), _const_spec((1, D)), _const_spec((1, D))],
        out_specs=pl.BlockSpec((tm, D), row),
        out_shape=jax.ShapeDtypeStruct((T, D), f32),
        scratch_shapes=[pltpu.VMEM((GATHER_SLOTS, TOP_K, tm, D), ys.dtype), pltpu.VMEM((tm, D), f32),
                        pltpu.SemaphoreType.DMA((GATHER_SLOTS,))],
        compiler_params=_params(("arbitrary",), 48),
        name="final",
    )(dest_blocks, dest_blocks, dest_blocks, x1, p2d, info, ys, w_up, w_gate, g_ple, g_final)


def _layer(x2d, p2d, g_mix, w_in, conv_w, conv_b, b_gate, gn_m, ln_g, ln_b, w_s, b_s, w_bm, w_bg, w_out,
           g_ffn, w_rg, b_rg, w_re, b_re, w1, w3, w2, g_ple, w_ple_up, w_ple_gate, g_out, *, B, S, final_norm):
    T, D = x2d.shape
    H = b_gate.shape[0] // 2
    W = gn_m.shape[0]
    GW = ln_g.shape[0]
    NG = w_rg.shape[1]
    NE = w_re.shape[1]
    EPG = NE // NG
    BM = EXPERT_BLOCK
    KW = conv_w.shape[0]
    tn = PROJ_COLS
    row2 = lambda v: v.reshape(1, -1).astype(f32)

    n_if = 2 * H
    c_if = 4 * W
    c_uv = c_if + n_if
    w_bf = w_in.astype(bf16)
    w_tail = w_bf[:, c_uv:]
    w_if = jnp.pad(w_bf[:, c_if:c_uv], ((0, 0), (0, LANES - n_if)))
    bg_pad = jnp.concatenate([b_gate.astype(f32), jnp.zeros((LANES - n_if,), f32)]).reshape(1, LANES)

    tm = min(PROJ_ROWS, S)
    hspec = lambda shape: pl.BlockSpec(shape, lambda i, j: (0, j))
    qk, hn = pl.pallas_call(
        functools.partial(_proj_qk_kernel, tiles_per_seq=S // tm, KW=KW, k_scale=(W // H) ** -0.5),
        grid=(T // tm, 2 * W // tn),
        in_specs=[pl.BlockSpec((tm, D), lambda i, j: (i, 0)), pl.BlockSpec((1, D), lambda i, j: (0, 0)),
                  hspec((D, tn)), hspec((KW, tn)), hspec((1, tn))],
        out_specs=[pl.BlockSpec((tm, tn), lambda i, j: (i, j)), pl.BlockSpec((tm, D), lambda i, j: (i, 0))],
        out_shape=[jax.ShapeDtypeStruct((T, 2 * W), bf16), jax.ShapeDtypeStruct((T, D), bf16)],
        scratch_shapes=[pltpu.VMEM((2 * W // tn, SUBLANES, tn), f32)],
        compiler_params=_params(("arbitrary", "arbitrary"), 52),
        name="proj_qk",
    )(x2d, row2(g_mix), w_bf, conv_w.astype(f32), row2(conv_b))
    tma = min(2 * PROJ_ROWS, T)
    v, pif = _proj("proj_v", _proj_v_kernel, hn, w_bf, 2 * W // tn, W // tn, tm=tma,
                   extra=(w_if,), extra_specs=(pl.BlockSpec((D, LANES), lambda i, j: (0, 0)),),
                   extra_out=(jax.ShapeDtypeStruct((T, LANES), f32),),
                   extra_out_specs=(pl.BlockSpec((tma, LANES), lambda i, j: (i, 0)),))
    so = _proj("proj_o", functools.partial(_proj_act_kernel, act=jax.nn.sigmoid), hn, w_bf, 3 * W // tn, W // tn, tm=tma)
    ug = _proj("proj_u", functools.partial(_proj_act_kernel, act=jax.nn.gelu), hn, w_tail, 0, GW // tn, tm=tma)
    vn = _proj("proj_vn", _proj_vn_kernel, hn, w_tail, GW // tn, 1, tm=tma,
               extra=(row2(ln_g), row2(ln_b)), extra_specs=(hspec((1, tn)), hspec((1, tn))),
               scratch=(pltpu.VMEM((tma, tn), f32),))
    gates = _proj("proj_gates", functools.partial(_proj_act_kernel, act=jax.nn.sigmoid), hn, w_tail,
                  2 * GW // tn, 2 * D // tn, tm=tma)

    h_m = _mlstm(qk, v, so, pif, bg_pad, row2(gn_m), B=B, S=S, H=H)

    w_r = jnp.concatenate([w_rg, w_re, jnp.zeros((D, LANES - NG - NE), w_rg.dtype)], axis=1).astype(bf16)
    b_r = jnp.concatenate([b_rg.astype(f32), b_re.astype(f32), jnp.zeros((LANES - NG - NE,), f32)]).reshape(1, LANES)
    x1, hn2, logits = _merge(h_m, ug, vn, gates, x2d, jnp.tril(w_s).astype(bf16), jnp.transpose(b_s).astype(f32),
                             w_bm.astype(bf16), w_bg.astype(bf16), w_out.astype(bf16), row2(g_ffn), w_r, b_r)

    info, cnt = _route(logits, NG=NG, EPG=EPG)
    counts = cnt[0, NG:NG + NE].astype(jnp.int32)
    padded = (counts + BM - 1) // BM * BM
    pends = jnp.cumsum(padded)
    pstarts = pends - padded
    NB = (T * TOP_K) // BM + NE
    blk_start = jnp.arange(NB, dtype=jnp.int32) * BM
    blk_e = jnp.minimum(jnp.sum((pends[None, :] <= blk_start[:, None]).astype(jnp.int32), axis=1), NE - 1)
    n_used = (pends[-1:] // BM).astype(jnp.int32)
    run_ends = (pends // BM).astype(jnp.int32)
    ps_row = jnp.zeros((1, LANES), f32).at[0, :NE].set(pstarts.astype(f32))
    seg = jnp.stack([jnp.concatenate([pstarts + counts, pends[-1:]]),
                     jnp.concatenate([pends, jnp.full((1,), NB * BM, jnp.int32)])]).astype(jnp.int32)

    dest = _dest(info, ps_row)
    tmd = min(256, T)
    dest_blocks = dest[:, :TOP_K].reshape(T // tmd, tmd, TOP_K).transpose(0, 2, 1)

    xs = _dispatch(seg, dest_blocks, hn2, NB * BM)
    ys = _experts(blk_e, n_used, run_ends, xs, w1, w3, w2)
    return _final(dest_blocks, x1, p2d, info, ys, w_ple_up.astype(bf16), w_ple_gate.astype(bf16),
                  row2(g_ple), row2(g_out), final_norm=final_norm)


def kernel(x, p, g_mix, w_in, conv_w, conv_b, b_gate, gn_m, ln_g, ln_b, w_s, b_s, w_bm, w_bg, w_out, g_ffn, w_rg, b_rg, w_re, b_re, w1, w3, w2, g_ple, w_ple_up, w_ple_gate, g_final):
    B, S, D = x.shape
    depth = w_in.shape[0]
    x2d = x.reshape(B * S, D)
    for i in range(depth):
        last = i == depth - 1
        x2d = _layer(x2d, p[i].reshape(B * S, -1), g_mix[i], w_in[i], conv_w[i], conv_b[i], b_gate[i], gn_m[i],
                     ln_g[i], ln_b[i], w_s[i], b_s[i], w_bm[i], w_bg[i], w_out[i], g_ffn[i], w_rg[i], b_rg[i],
                     w_re[i], b_re[i], w1[i], w3[i], w2[i], g_ple[i], w_ple_up[i], w_ple_gate[i], g_final,
                     B=B, S=S, final_norm=last)
    return x2d.reshape(B, S, D)
```

```python
import functools

import jax
import jax.numpy as jnp
from jax import lax
from jax.experimental import pallas as pl
from jax.experimental.pallas import tpu as pltpu

EPS = 1e-6
M_CHUNK = 128
TOP_K = 2
LANES = 128
SUBLANES = 8
MXU_DIM = 256
EXPERT_BLOCK = 256
PROJ_ROWS = 1024
PROJ_COLS = 1024
MIB = 1024 * 1024

f32 = jnp.float32
bf16 = jnp.bfloat16


def _params(semantics, vmem_mib):
    return pltpu.CompilerParams(dimension_semantics=semantics, vmem_limit_bytes=vmem_mib * MIB)


def _const_spec(shape):
    nd = len(shape)
    return pl.BlockSpec(shape, lambda *_: (0,) * nd, pipeline_mode=pl.Buffered(1))


def _rms(x, g):
    ms = jnp.mean(x * x, axis=-1, keepdims=True)
    return x * lax.rsqrt(ms + EPS) * g


def _chunk_cols(n):
    return slice(n * MXU_DIM, (n + 1) * MXU_DIM)


def _dot_chunk(hn_ref, w_ref, n):
    return jnp.dot(hn_ref[...], w_ref[:, _chunk_cols(n)], preferred_element_type=f32)


def _proj_act_kernel(hn_ref, w_ref, o_ref, *, act):
    for n in range(o_ref.shape[1] // MXU_DIM):
        o_ref[:, _chunk_cols(n)] = act(_dot_chunk(hn_ref, w_ref, n)).astype(o_ref.dtype)


def _proj_v_kernel(hn_ref, w_ref, wif_ref, o_ref, oif_ref):
    _proj_act_kernel(hn_ref, w_ref, o_ref, act=lambda t: t)
    oif_ref[...] = jnp.dot(hn_ref[...], wif_ref[...], preferred_element_type=f32)


def _proj_qk_kernel(x_ref, g_ref, w_ref, cw_ref, cb_ref, o_ref, hn_ref, carry_ref, *, tiles_per_seq, KW, k_scale):
    i = pl.program_id(0)
    j = pl.program_id(1)
    tm = hn_ref.shape[0]
    scale = jnp.where(j == 1, k_scale, 1.0).astype(f32)

    @pl.when(j == 0)
    def _():
        hn_ref[...] = _rms(x_ref[...], g_ref[...]).astype(hn_ref.dtype)

    @pl.when(lax.rem(i, tiles_per_seq) == 0)
    def _():
        carry_ref[j] = jnp.zeros(carry_ref.shape[1:], f32)

    for n in range(o_ref.shape[1] // MXU_DIM):
        cols = _chunk_cols(n)
        acc = _dot_chunk(hn_ref, w_ref, n)
        prev = carry_ref[j, :, cols]
        carry_ref[j, :, cols] = acc[tm - SUBLANES:, :]
        ext = jnp.concatenate([prev, acc], axis=0)
        y = cb_ref[:, cols] + cw_ref[KW - 1:KW, cols] * acc
        for s in range(1, KW):
            y = y + cw_ref[KW - 1 - s:KW - s, cols] * pltpu.roll(ext, s, axis=0)[SUBLANES:, :]
        o_ref[:, cols] = (y * jax.nn.sigmoid(y) * scale).astype(o_ref.dtype)


def _proj_vn_kernel(hn_ref, w_ref, lng_ref, lnb_ref, o_ref, g_ref):
    nch = o_ref.shape[1] // MXU_DIM
    width = o_ref.shape[1]
    total = 0.0
    for n in range(nch):
        g = jax.nn.gelu(_dot_chunk(hn_ref, w_ref, n))
        g_ref[:, _chunk_cols(n)] = g
        total = total + jnp.sum(g, axis=-1, keepdims=True)
    mu = total / width
    sq = 0.0
    for n in range(nch):
        xc = g_ref[:, _chunk_cols(n)] - mu
        sq = sq + jnp.sum(xc * xc, axis=-1, keepdims=True)
    inv = lax.rsqrt(sq / width + EPS)
    for n in range(nch):
        cols = _chunk_cols(n)
        o_ref[:, cols] = ((g_ref[:, cols] - mu) * inv * lng_ref[:, cols] + lnb_ref[:, cols]).astype(o_ref.dtype)


def _proj(name, body, hn, w, col0, ncol, *, extra=(), extra_specs=(), extra_out=(), extra_out_specs=(), scratch=(),
          tm):
    T, D = hn.shape
    tn = PROJ_COLS
    outs = pl.pallas_call(
        body,
        grid=(T // tm, ncol),
        in_specs=[pl.BlockSpec((tm, D), lambda i, j: (i, 0)),
                  pl.BlockSpec((D, tn), lambda i, j: (0, col0 + j))] + list(extra_specs),
        out_specs=[pl.BlockSpec((tm, tn), lambda i, j: (i, j))] + list(extra_out_specs),
        out_shape=[jax.ShapeDtypeStruct((T, ncol * tn), bf16)] + list(extra_out),
        scratch_shapes=list(scratch),
        compiler_params=_params(("arbitrary", "arbitrary"), 48),
        name=name,
    )(hn, w, *extra)
    return outs if extra_out else outs[0]


def _mlstm_kernel(q_ref, k_ref, v_ref, so_ref, if_ref, bg_ref, gn_ref, out_ref, c_ref, m_ref, *, H, Dh):
    @pl.when(pl.program_id(1) == 0)
    def _():
        c_ref[...] = jnp.zeros_like(c_ref)
        m_ref[...] = jnp.zeros_like(m_ref)

    for bb in range(q_ref.shape[0]):
        _mlstm_chunk(q_ref.at[bb], k_ref.at[bb], v_ref.at[bb], so_ref.at[bb], if_ref.at[bb], bg_ref, gn_ref,
                     out_ref.at[bb], c_ref.at[bb], m_ref.at[bb], H=H, Dh=Dh)


def _mlstm_chunk(q_ref, k_ref, v_ref, so_ref, if_ref, bg_ref, gn_ref, out_ref, c_ref, m_ref, *, H, Dh):
    L = M_CHUNK
    gp = if_ref[...] + bg_ref[...]
    lane = lax.broadcasted_iota(jnp.int32, (L, LANES), 1)
    lf = jnp.minimum(gp, 0.0) - jnp.log1p(jnp.exp(-jnp.abs(gp)))
    z = jnp.where(lane < H, gp, lf)
    row = lax.broadcasted_iota(jnp.int32, (L, L), 0)
    col = lax.broadcasted_iota(jnp.int32, (L, L), 1)
    causal = row >= col
    bc = jnp.dot(causal.astype(f32), z, precision=lax.Precision.HIGHEST, preferred_element_type=f32)
    zt = z.T
    bct = bc.T

    ones = jnp.ones((L, LANES), bf16)
    heads = range(H)
    sl = [slice(h * Dh, (h + 1) * Dh) for h in heads]
    qb = [q_ref[:, sl[h]] for h in heads]
    kb = [k_ref[:, sl[h]] for h in heads]
    va = [jnp.concatenate([v_ref[:, sl[h]], ones], axis=1) for h in heads]
    b_col = [bc[:, H + h:H + h + 1] for h in heads]
    g = [b_col[h][L - 1:L, :] for h in heads]
    m_prev = [m_ref[h, 0:1, 0:1] for h in heads]

    a_col = [g[h] - b_col[h] + z[:, h:h + 1] for h in heads]
    m_loc = [jnp.max(a_col[h], axis=0, keepdims=True) for h in heads]
    w_loc = [jnp.exp(a_col[h] - m_loc[h]) for h in heads]
    log_d = [jnp.where(causal, b_col[h] - bct[H + h:H + h + 1, :] + zt[h:h + 1, :], -jnp.inf) for h in heads]
    log_inter = [b_col[h] + m_prev[h] for h in heads]
    mt = [jnp.maximum(log_inter[h], jnp.max(log_d[h], axis=-1, keepdims=True)) for h in heads]
    d = [jnp.exp(log_d[h] - mt[h]) for h in heads]
    s_inter = [jnp.exp(log_inter[h] - mt[h]) for h in heads]

    s = [lax.dot_general(qb[h], kb[h], (((1,), (1,)), ((), ())), preferred_element_type=f32) * d[h] for h in heads]
    inter = [jnp.dot(qb[h], c_ref[h].astype(bf16), preferred_element_type=f32) for h in heads]
    nd = [jnp.dot(s[h].astype(bf16), va[h], preferred_element_type=f32) + s_inter[h] * inter[h]
          for h in heads]
    hh = [nd[h][:, :Dh] / jnp.maximum(jnp.abs(nd[h][:, Dh:Dh + 1]), jnp.exp(-mt[h])) for h in heads]

    m_new = [jnp.maximum(g[h] + m_prev[h], m_loc[h]) for h in heads]
    kv = [jnp.dot((kb[h].astype(f32) * w_loc[h]).T.astype(bf16), va[h], preferred_element_type=f32) for h in heads]
    for h in heads:
        c_ref[h] = jnp.exp(g[h] + m_prev[h] - m_new[h]) * c_ref[h] + jnp.exp(m_loc[h] - m_new[h]) * kv[h]
        m_ref[h] = jnp.broadcast_to(m_new[h], m_ref.shape[1:])

    mu = [jnp.mean(hh[h], axis=-1, keepdims=True) for h in heads]
    xc = [hh[h] - mu[h] for h in heads]
    var = [jnp.mean(xc[h] * xc[h], axis=-1, keepdims=True) for h in heads]
    for h in heads:
        hn = xc[h] * lax.rsqrt(var[h] + EPS) * gn_ref[:, sl[h]]
        out_ref[:, sl[h]] = (so_ref[:, sl[h]].astype(f32) * hn).astype(out_ref.dtype)


def _mlstm(qk, v, so, pif, bg_pad, gn_m, *, B, S, H):
    T, W = v.shape
    L = M_CHUNK
    Dh = W // H
    BB = 1
    seq = lambda a: a.reshape(B, S, a.shape[-1])
    blk = lambda width, cb: pl.BlockSpec((BB, L, width), lambda b, c: (b, c, cb))
    out = pl.pallas_call(
        functools.partial(_mlstm_kernel, H=H, Dh=Dh),
        grid=(B // BB, S // L),
        in_specs=[blk(W, 0), blk(W, 1), blk(W, 0), blk(W, 0), blk(LANES, 0),
                  pl.BlockSpec((1, LANES), lambda b, c: (0, 0)),
                  pl.BlockSpec((1, W), lambda b, c: (0, 0))],
        out_specs=blk(W, 0),
        out_shape=jax.ShapeDtypeStruct((B, S, W), bf16),
        scratch_shapes=[pltpu.VMEM((BB, H, Dh, Dh + LANES), f32),
                        pltpu.VMEM((BB, H, SUBLANES, LANES), f32)],
        compiler_params=_params(("parallel", "arbitrary"), 32),
        name="mlstm",
    )(seq(qk), seq(qk), seq(v), seq(so), seq(pif), bg_pad, gn_m)
    return out.reshape(T, W)


def _merge_kernel(hm_ref, ug_ref, vn_ref, gm_ref, gg_ref, x_ref, ws_ref, bst_ref, wbm_ref, wbg_ref, wout_ref,
                  gffn_ref, wr_ref, br_ref, x1_ref, hn2_ref, lg_ref, hg_ref):
    tm = x_ref.shape[0]
    G, L, _ = ws_ref.shape
    Gd = ug_ref.shape[1] // G
    for c in range(tm // L):
        rows = slice(c * L, (c + 1) * L)
        for g in range(G):
            cols = slice(g * Gd, (g + 1) * Gd)
            mixed = jnp.dot(ws_ref[g], vn_ref[rows, cols], preferred_element_type=f32) + bst_ref[:, g:g + 1]
            hg_ref[rows, cols] = (ug_ref[rows, cols].astype(f32) * mixed).astype(hg_ref.dtype)

    a = jnp.dot(hm_ref[...], wbm_ref[...], preferred_element_type=f32)
    b = jnp.dot(hg_ref[...], wbg_ref[...], preferred_element_type=f32)
    merged = gm_ref[...].astype(f32) * a + gg_ref[...].astype(f32) * b
    x1 = x_ref[...] + jnp.dot(merged.astype(bf16), wout_ref[...], preferred_element_type=f32)
    x1_ref[...] = x1
    hn2 = _rms(x1, gffn_ref[...])
    hn2_ref[...] = hn2
    lg_ref[...] = jnp.dot(hn2.astype(bf16), wr_ref[...], preferred_element_type=f32) + br_ref[...]


def _merge(h_m, ug, vn, gates, x2d, ws_tril, b_st, w_bm, w_bg, w_out, g_ffn, w_r, b_r):
    T, D = x2d.shape
    W = h_m.shape[1]
    GW = ug.shape[1]
    G, L, _ = ws_tril.shape
    tm = min(256, T)
    row = lambda i: (i, 0)
    return pl.pallas_call(
        _merge_kernel,
        grid=(T // tm,),
        in_specs=[pl.BlockSpec((tm, W), row),
                  pl.BlockSpec((tm, GW), row),
                  pl.BlockSpec((tm, GW), row),
                  pl.BlockSpec((tm, D), lambda i: (i, 0)),
                  pl.BlockSpec((tm, D), lambda i: (i, 1)),
                  pl.BlockSpec((tm, D), row),
                  _const_spec((G, L, L)), _const_spec((L, G)),
                  _const_spec((W, D)), _const_spec((GW, D)), _const_spec((D, D)),
                  _const_spec((1, D)), _const_spec((D, LANES)), _const_spec((1, LANES))],
        out_specs=[pl.BlockSpec((tm, D), row), pl.BlockSpec((tm, D), row), pl.BlockSpec((tm, LANES), row)],
        out_shape=[jax.ShapeDtypeStruct((T, D), f32), jax.ShapeDtypeStruct((T, D), f32),
                   jax.ShapeDtypeStruct((T, LANES), f32)],
        scratch_shapes=[pltpu.VMEM((tm, GW), bf16)],
        compiler_params=_params(("parallel",), 48),
        name="merge",
    )(h_m, ug, vn, gates, gates, x2d, ws_tril, b_st, w_bm, w_bg, w_out, g_ffn, w_r, b_r)


def _route_kernel(lg_ref, info_ref, cnt_ref, carry_ref, *, NG, EPG):
    tm = lg_ref.shape[0]

    @pl.when(pl.program_id(0) == 0)
    def _():
        carry_ref[...] = jnp.zeros_like(carry_ref)

    lg = lg_ref[...]
    lane = lax.broadcasted_iota(jnp.int32, (tm, LANES), 1)
    lanef = lane.astype(f32)
    big = float(LANES)

    def first_max(vals):
        mx = jnp.max(vals, axis=-1, keepdims=True)
        idx = jnp.min(jnp.where(vals == mx, lanef, big), axis=-1, keepdims=True)
        return mx, idx

    is_g = lane < NG
    gmax, grp = first_max(jnp.where(is_g, lg, -jnp.inf))
    p_grp = 1.0 / jnp.sum(jnp.where(is_g, jnp.exp(lg - gmax), 0.0), axis=-1, keepdims=True)
    lo = NG + grp * EPG
    el = jnp.where(jnp.logical_and(lanef >= lo, lanef < lo + EPG), lg, -jnp.inf)
    v1, i1 = first_max(el)
    v2, i2 = first_max(jnp.where(lanef == i1, -jnp.inf, el))
    t = jnp.exp(v2 - v1)
    w1 = p_grp / (1.0 + t)
    w2 = p_grp * t / (1.0 + t)

    hit1 = lanef == i1
    hit2 = lanef == i2
    oh = jnp.where(jnp.logical_or(hit1, hit2), 1.0, 0.0)
    r = lax.broadcasted_iota(jnp.int32, (tm, tm), 0)
    c = lax.broadcasted_iota(jnp.int32, (tm, tm), 1)
    before = jnp.where(r > c, 1.0, 0.0).astype(bf16)
    carry = carry_ref[0:1, :]
    earlier = jnp.dot(before, oh.astype(bf16), preferred_element_type=f32) + carry
    rank1 = jnp.sum(jnp.where(hit1, earlier, 0.0), axis=-1, keepdims=True)
    rank2 = jnp.sum(jnp.where(hit2, earlier, 0.0), axis=-1, keepdims=True)
    new_carry = carry + jnp.sum(oh, axis=0, keepdims=True)
    carry_ref[...] = jnp.broadcast_to(new_carry, carry_ref.shape)
    cnt_ref[...] = jnp.broadcast_to(new_carry, cnt_ref.shape)

    vals = (i1 - NG, i2 - NG, w1, w2, rank1, rank2)
    info = jnp.zeros((tm, LANES), f32)
    for k, v in enumerate(vals):
        info = jnp.where(lane == k, v, info)
    info_ref[...] = info


def _route(logits, *, NG, EPG):
    T = logits.shape[0]
    tm = min(512, T)
    return pl.pallas_call(
        functools.partial(_route_kernel, NG=NG, EPG=EPG),
        grid=(T // tm,),
        in_specs=[pl.BlockSpec((tm, LANES), lambda i: (i, 0))],
        out_specs=[pl.BlockSpec((tm, LANES), lambda i: (i, 0)),
                   pl.BlockSpec((SUBLANES, LANES), lambda i: (0, 0))],
        out_shape=[jax.ShapeDtypeStruct((T, LANES), f32), jax.ShapeDtypeStruct((SUBLANES, LANES), f32)],
        scratch_shapes=[pltpu.VMEM((SUBLANES, LANES), f32)],
        compiler_params=_params(("arbitrary",), 32),
        name="route",
    )(logits)


def _dest_kernel(info_ref, ps_ref, dest_ref):
    tm = info_ref.shape[0]
    info = info_ref[...]
    lane = lax.broadcasted_iota(jnp.int32, (tm, LANES), 1)
    lanef = lane.astype(f32)
    ps = ps_ref[...]
    d1 = jnp.sum(jnp.where(lanef == info[:, 0:1], ps, 0.0), axis=-1, keepdims=True) + info[:, 4:5]
    d2 = jnp.sum(jnp.where(lanef == info[:, 1:2], ps, 0.0), axis=-1, keepdims=True) + info[:, 5:6]
    dest_ref[...] = jnp.where(lane == 0, d1, jnp.where(lane == 1, d2, 0.0)).astype(jnp.int32)


def _dest(info, pstart_row):
    T = info.shape[0]
    tm = min(512, T)
    return pl.pallas_call(
        _dest_kernel,
        grid=(T // tm,),
        in_specs=[pl.BlockSpec((tm, LANES), lambda i: (i, 0)), pl.BlockSpec((1, LANES), lambda i: (0, 0))],
        out_specs=pl.BlockSpec((tm, LANES), lambda i: (i, 0)),
        out_shape=jax.ShapeDtypeStruct((T, LANES), jnp.int32),
        compiler_params=_params(("parallel",), 32),
        name="dest",
    )(info, pstart_row)


def _row_copy(src, si, dst, di, sem, n=1):
    return pltpu.make_async_copy(src.at[pl.ds(si, n)], dst.at[pl.ds(di, n)], sem)


def _fill_rows(seg_ref, zero_ref, xs_ref, sem, *, wait):
    zrows = zero_ref.shape[0]

    def act(start, n):
        cp = _row_copy(zero_ref, 0, xs_ref, start, sem, n)
        cp.wait() if wait else cp.start()

    def aligned(start, n):
        act(pl.multiple_of(start, SUBLANES), n)

    def per_segment(e, carry):
        lo = seg_ref[0, e]
        hi = seg_ref[1, e]
        lo_al = jnp.minimum((lo + SUBLANES - 1) // SUBLANES * SUBLANES, hi)

        def single(r, cc):
            act(r, 1)
            return cc

        lax.fori_loop(lo, lo_al, single, 0)
        count = hi - lo_al
        nfull = count // zrows

        def full(c, cc):
            aligned(lo_al + c * zrows, zrows)
            return cc

        lax.fori_loop(0, nfull, full, 0)
        pos = lo_al + nfull * zrows
        rem = count - nfull * zrows
        n = zrows // 2
        while n >= SUBLANES:
            take = (rem & n) != 0
            pl.when(take)(functools.partial(aligned, pos, n))
            pos = pos + jnp.where(take, n, 0)
            n //= 2
        return carry

    lax.fori_loop(0, seg_ref.shape[1], per_segment, 0)


def _dispatch_kernel(seg_ref, dest_ref, hn_ref, xs_ref, zero_ref, sem, zsem):
    tm = hn_ref.shape[0]

    @pl.when(pl.program_id(0) == 0)
    def _():
        zero_ref[...] = jnp.zeros_like(zero_ref)
        _fill_rows(seg_ref, zero_ref, xs_ref, zsem, wait=False)
        _fill_rows(seg_ref, zero_ref, xs_ref, zsem, wait=True)

    for r in range(tm):
        for k in range(TOP_K):
            _row_copy(hn_ref, r, xs_ref, dest_ref[0, k, r], sem).start(priority=k % 2)

    def drain(r, carry):
        for k in range(TOP_K):
            _row_copy(hn_ref, 0, xs_ref, 0, sem).wait()
        return carry

    lax.fori_loop(0, tm, drain, 0, unroll=8)


def _dispatch(seg, dest_blocks, hn2, n_rows):
    T, D = hn2.shape
    tm = dest_blocks.shape[2]
    grid_spec = pltpu.PrefetchScalarGridSpec(
        num_scalar_prefetch=1,
        grid=(T // tm,),
        in_specs=[pl.BlockSpec((1, TOP_K, tm), lambda i, seg: (i, 0, 0), memory_space=pltpu.SMEM),
                  pl.BlockSpec((tm, D), lambda i, seg: (i, 0))],
        out_specs=pl.BlockSpec(memory_space=pl.ANY),
        scratch_shapes=[pltpu.VMEM((EXPERT_BLOCK // 2, D), hn2.dtype), pltpu.SemaphoreType.DMA(()),
                        pltpu.SemaphoreType.DMA(())],
    )
    return pl.pallas_call(
        _dispatch_kernel,
        grid_spec=grid_spec,
        out_shape=jax.ShapeDtypeStruct((n_rows, D), hn2.dtype),
        compiler_params=_params(("arbitrary",), 32),
        name="dispatch",
    )(seg, dest_blocks, hn2)


def _experts_kernel(be_ref, nu_ref, re_ref, xs_ref, w1_ref, w3_ref, w2_ref, ys_ref, st1, st3, st2, wb1, wb3, wb2, sems):
    i = pl.program_id(0)
    e = be_ref[i]
    prev = be_ref[jnp.maximum(i - 1, 0)]
    run_end = re_ref[e]

    def fetch(ex):
        return [pltpu.make_async_copy(w1_ref.at[ex], st1, sems.at[0]),
                pltpu.make_async_copy(w3_ref.at[ex], st3, sems.at[1]),
                pltpu.make_async_copy(w2_ref.at[ex], st2, sems.at[2])]

    @pl.when(i == 0)
    def _():
        for cp in fetch(e):
            cp.start()

    @pl.when(jnp.logical_or(i == 0, jnp.logical_and(e != prev, i < nu_ref[0])))
    def _():
        for cp in fetch(e):
            cp.wait()
        wb1[...] = st1[...].astype(bf16)
        wb3[...] = st3[...].astype(bf16)
        wb2[...] = st2[...].astype(bf16)
        @pl.when(run_end < nu_ref[0])
        def _():
            for cp in fetch(be_ref[jnp.minimum(run_end, pl.num_programs(0) - 1)]):
                cp.start(priority=1)

    @pl.when(i < nu_ref[0])
    def _():
        x = xs_ref[...].astype(bf16)
        h1 = jnp.dot(x, wb1[...], preferred_element_type=f32)
        h3 = jnp.dot(x, wb3[...], preferred_element_type=f32)
        a = (h1 * jax.nn.sigmoid(h1) * h3).astype(bf16)
        ys_ref[...] = jnp.dot(a, wb2[...], preferred_element_type=f32)

    @pl.when(i >= nu_ref[0])
    def _():
        ys_ref[...] = jnp.zeros_like(ys_ref)


def _experts(blk_e, n_used, run_ends, xs, w1, w3, w2):
    P, D = xs.shape
    NE, _, De = w1.shape
    BM = EXPERT_BLOCK
    grid_spec = pltpu.PrefetchScalarGridSpec(
        num_scalar_prefetch=3,
        grid=(P // BM,),
        in_specs=[pl.BlockSpec((BM, D), lambda i, be, nu, nx: (jnp.minimum(i, nu[0] - 1), 0)),
                  pl.BlockSpec(memory_space=pl.ANY), pl.BlockSpec(memory_space=pl.ANY),
                  pl.BlockSpec(memory_space=pl.ANY)],
        out_specs=pl.BlockSpec((BM, D), lambda i, be, nu, nx: (i, 0)),
        scratch_shapes=[pltpu.VMEM((D, De), w1.dtype), pltpu.VMEM((D, De), w3.dtype), pltpu.VMEM((De, D), w2.dtype),
                        pltpu.VMEM((D, De), bf16), pltpu.VMEM((D, De), bf16), pltpu.VMEM((De, D), bf16),
                        pltpu.SemaphoreType.DMA((3,))],
    )
    return pl.pallas_call(
        _experts_kernel,
        grid_spec=grid_spec,
        out_shape=jax.ShapeDtypeStruct((P, D), f32),
        compiler_params=_params(("arbitrary",), 48),
        name="experts",
    )(blk_e, n_used, run_ends, xs, w1, w3, w2)


GATHER_SLOTS = 3


def _final_kernel(dcur_ref, dnx1_ref, dnx2_ref, x1_ref, p_ref, info_ref, ys_ref, wup_ref, wg_ref, gple_ref, gfin_ref,
                  out_ref, buf_ref, x2_ref, sems, *, final_norm):
    tm = x1_ref.shape[0]
    i = pl.program_id(0)
    slot = lax.rem(i, GATHER_SLOTS)
    slot1 = lax.rem(i + 1, GATHER_SLOTS)
    slot2 = lax.rem(i + 2, GATHER_SLOTS)

    def gather(dref, r, s):
        return [_row_copy(ys_ref, dref[0, k, r], buf_ref.at[s, k], r, sems.at[s]) for k in range(TOP_K)]

    def drain(s):
        def body(r, carry):
            for cp in gather(dcur_ref, 0, s):
                cp.wait()
            return carry
        lax.fori_loop(0, tm, body, 0, unroll=8)

    @pl.when(i == 0)
    def _():
        def body(r, carry):
            for cp in gather(dcur_ref, r, 0) + gather(dnx1_ref, r, 1):
                cp.start()
            return carry
        lax.fori_loop(0, tm, body, 0, unroll=8)

    drain(slot)
    info = info_ref[...]
    x2_ref[...] = x1_ref[...] + buf_ref[slot, 0] * info[:, 2:3] + buf_ref[slot, 1] * info[:, 3:4]

    for r in range(tm):
        for k, cp in enumerate(gather(dnx2_ref, r, slot2)):
            cp.start(priority=k % 2)

    ple = jnp.dot(p_ref[...].astype(bf16), wup_ref[...], preferred_element_type=f32)
    x2 = x2_ref[...]
    hn3 = _rms(x2, gple_ref[...]).astype(bf16)
    gate = jax.nn.sigmoid(jnp.dot(hn3, wg_ref[...], preferred_element_type=f32))
    x3 = x2 + gate * ple
    out_ref[...] = _rms(x3, gfin_ref[...]) if final_norm else x3

    @pl.when(i == pl.num_programs(0) - 1)
    def _():
        drain(slot1)
        drain(slot2)


def _final(dest_blocks, x1, p2d, info, ys, w_up, w_gate, g_ple, g_final, *, final_norm):
    T, D = x1.shape
    PD = p2d.shape[1]
    tm = dest_blocks.shape[2]
    nt = T // tm
    row = lambda i: (i, 0)

    def dest_spec(ahead):
        return pl.BlockSpec((1, TOP_K, tm), lambda i: (jnp.minimum(i + ahead, nt - 1), 0, 0),
                            memory_space=pltpu.SMEM)

    return pl.pallas_call(
        functools.partial(_final_kernel, final_norm=final_norm),
        grid=(nt,),
        in_specs=[dest_spec(a) for a in range(GATHER_SLOTS)] + [
                  pl.BlockSpec((tm, D), row),
                  pl.BlockSpec((tm, PD), row),
                  pl.BlockSpec((tm, LANES), row),
                  pl.BlockSpec(memory_space=pl.ANY),
                  _const_spec((PD, D)), _const_spec((D, D)), _const_spec((1, D)), _const_spec((1, D))],
        out_specs=pl.BlockSpec((tm, D), row),
        out_shape=jax.ShapeDtypeStruct((T, D), f32),
        scratch_shapes=[pltpu.VMEM((GATHER_SLOTS, TOP_K, tm, D), ys.dtype), pltpu.VMEM((tm, D), f32),
                        pltpu.SemaphoreType.DMA((GATHER_SLOTS,))],
        compiler_params=_params(("arbitrary",), 48),
        name="final",
    )(dest_blocks, dest_blocks, dest_blocks, x1, p2d, info, ys, w_up, w_gate, g_ple, g_final)


def _layer(x2d, p2d, g_mix, w_in, conv_w, conv_b, b_gate, gn_m, ln_g, ln_b, w_s, b_s, w_bm, w_bg, w_out,
           g_ffn, w_rg, b_rg, w_re, b_re, w1, w3, w2, g_ple, w_ple_up, w_ple_gate, g_out, *, B, S, final_norm):
    T, D = x2d.shape
    H = b_gate.shape[0] // 2
    W = gn_m.shape[0]
    GW = ln_g.shape[0]
    NG = w_rg.shape[1]
    NE = w_re.shape[1]
    EPG = NE // NG
    BM = EXPERT_BLOCK
    KW = conv_w.shape[0]
    tn = PROJ_COLS
    row2 = lambda v: v.reshape(1, -1).astype(f32)

    n_if = 2 * H
    c_if = 4 * W
    c_uv = c_if + n_if
    w_bf = w_in.astype(bf16)
    w_tail = w_bf[:, c_uv:]
    w_if = jnp.pad(w_bf[:, c_if:c_uv], ((0, 0), (0, LANES - n_if)))
    bg_pad = jnp.concatenate([b_gate.astype(f32), jnp.zeros((LANES - n_if,), f32)]).reshape(1, LANES)

    tm = min(PROJ_ROWS, S)
    hspec = lambda shape: pl.BlockSpec(shape, lambda i, j: (0, j))
    qk, hn = pl.pallas_call(
        functools.partial(_proj_qk_kernel, tiles_per_seq=S // tm, KW=KW, k_scale=(W // H) ** -0.5),
        grid=(T // tm, 2 * W // tn),
        in_specs=[pl.BlockSpec((tm, D), lambda i, j: (i, 0)), pl.BlockSpec((1, D), lambda i, j: (0, 0)),
                  hspec((D, tn)), hspec((KW, tn)), hspec((1, tn))],
        out_specs=[pl.BlockSpec((tm, tn), lambda i, j: (i, j)), pl.BlockSpec((tm, D), lambda i, j: (i, 0))],
        out_shape=[jax.ShapeDtypeStruct((T, 2 * W), bf16), jax.ShapeDtypeStruct((T, D), bf16)],
        scratch_shapes=[pltpu.VMEM((2 * W // tn, SUBLANES, tn), f32)],
        compiler_params=_params(("arbitrary", "arbitrary"), 52),
        name="proj_qk",
    )(x2d, row2(g_mix), w_bf, conv_w.astype(f32), row2(conv_b))
    v, pif = _proj("proj_v", _proj_v_kernel, hn, w_bf, 2 * W // tn, W // tn, tm=tm,
                   extra=(w_if,), extra_specs=(pl.BlockSpec((D, LANES), lambda i, j: (0, 0)),),
                   extra_out=(jax.ShapeDtypeStruct((T, LANES), f32),),
                   extra_out_specs=(pl.BlockSpec((tm, LANES), lambda i, j: (i, 0)),))
    so = _proj("proj_o", functools.partial(_proj_act_kernel, act=jax.nn.sigmoid), hn, w_bf, 3 * W // tn, W // tn, tm=tm)
    ug = _proj("proj_u", functools.partial(_proj_act_kernel, act=jax.nn.gelu), hn, w_tail, 0, GW // tn, tm=tm)
    vn = _proj("proj_vn", _proj_vn_kernel, hn, w_tail, GW // tn, 1, tm=tm,
               extra=(row2(ln_g), row2(ln_b)), extra_specs=(hspec((1, tn)), hspec((1, tn))),
               scratch=(pltpu.VMEM((tm, tn), f32),))
    gates = _proj("proj_gates", functools.partial(_proj_act_kernel, act=jax.nn.sigmoid), hn, w_tail,
                  2 * GW // tn, 2 * D // tn, tm=tm)

    h_m = _mlstm(qk, v, so, pif, bg_pad, row2(gn_m), B=B, S=S, H=H)

    w_r = jnp.concatenate([w_rg, w_re, jnp.zeros((D, LANES - NG - NE), w_rg.dtype)], axis=1).astype(bf16)
    b_r = jnp.concatenate([b_rg.astype(f32), b_re.astype(f32), jnp.zeros((LANES - NG - NE,), f32)]).reshape(1, LANES)
    x1, hn2, logits = _merge(h_m, ug, vn, gates, x2d, jnp.tril(w_s).astype(bf16), jnp.transpose(b_s).astype(f32),
                             w_bm.astype(bf16), w_bg.astype(bf16), w_out.astype(bf16), row2(g_ffn), w_r, b_r)

    info, cnt = _route(logits, NG=NG, EPG=EPG)
    counts = cnt[0, NG:NG + NE].astype(jnp.int32)
    padded = (counts + BM - 1) // BM * BM
    pends = jnp.cumsum(padded)
    pstarts = pends - padded
    NB = (T * TOP_K) // BM + NE
    blk_start = jnp.arange(NB, dtype=jnp.int32) * BM
    blk_e = jnp.minimum(jnp.sum((pends[None, :] <= blk_start[:, None]).astype(jnp.int32), axis=1), NE - 1)
    n_used = (pends[-1:] // BM).astype(jnp.int32)
    run_ends = (pends // BM).astype(jnp.int32)
    ps_row = jnp.zeros((1, LANES), f32).at[0, :NE].set(pstarts.astype(f32))
    seg = jnp.stack([jnp.concatenate([pstarts + counts, pends[-1:]]),
                     jnp.concatenate([pends, jnp.full((1,), NB * BM, jnp.int32)])]).astype(jnp.int32)

    dest = _dest(info, ps_row)
    tmd = min(256, T)
    dest_blocks = dest[:, :TOP_K].reshape(T // tmd, tmd, TOP_K).transpose(0, 2, 1)

    xs = _dispatch(seg, dest_blocks, hn2, NB * BM)
    ys = _experts(blk_e, n_used, run_ends, xs, w1, w3, w2)
    return _final(dest_blocks, x1, p2d, info, ys, w_ple_up.astype(bf16), w_ple_gate.astype(bf16),
                  row2(g_ple), row2(g_out), final_norm=final_norm)


def kernel(x, p, g_mix, w_in, conv_w, conv_b, b_gate, gn_m, ln_g, ln_b, w_s, b_s, w_bm, w_bg, w_out, g_ffn, w_rg, b_rg, w_re, b_re, w1, w3, w2, g_ple, w_ple_up, w_ple_gate, g_final):
    B, S, D = x.shape
    depth = w_in.shape[0]
    x2d = x.reshape(B * S, D)
    for i in range(depth):
        last = i == depth - 1
        x2d = _layer(x2d, p[i].reshape(B * S, -1), g_mix[i], w_in[i], conv_w[i], conv_b[i], b_gate[i], gn_m[i],
                     ln_g[i], ln_b[i], w_s[i], b_s[i], w_bm[i], w_bg[i], w_out[i], g_ffn[i], w_rg[i], b_rg[i],
                     w_re[i], b_re[i], w1[i], w3[i], w2[i], g_ple[i], w_ple_up[i], w_ple_gate[i], g_final,
                     B=B, S=S, final_norm=last)
    return x2d.reshape(B, S, D)
```
